```python
import math
import jax, jax.numpy as jnp
from jax import lax
import numpy as np

D_MODEL = 1024
BATCH = 16
SEQ = 2048
DEPTH = 1
DEC_BATCH = 32
DEC_SEQ = 64
PAST_LEN = 2048

CHUNK = 64
N_HEADS = 8
HEAD_DIM = D_MODEL // N_HEADS
DK = HEAD_DIM // 2
SCALE = DK ** -0.5
N_BUCKETS = 32
MAX_DIST = 128
Q_BLOCK = 128
GM_CHUNK = 128
GM_GROUPS = 8
GM_WIDTH = D_MODEL
GM_GROUP_DIM = GM_WIDTH // GM_GROUPS
D_FF = ((8 * D_MODEL // 3 + 255) // 256) * 256
Q_W = N_HEADS * 2 * DK
K_W = N_HEADS * 2 * DK
V_W = N_HEADS * HEAD_DIM
IN_SIZES = [Q_W, K_W, V_W, GM_WIDTH, GM_WIDTH, D_MODEL, D_MODEL]
IN_W = sum(IN_SIZES)
SPLITS = [int(s) for s in np.cumsum(IN_SIZES)[:-1]]
EPS = 1e-6
NEG = -1e30

kernel_name = "hybrid_diffattn_gmlp_stream_step"


def rms_norm(x, g):
    xf = x.astype(jnp.float32)
    y = xf * lax.rsqrt(jnp.mean(xf * xf, axis=-1, keepdims=True) + EPS)
    return (y * g.astype(jnp.float32)).astype(x.dtype)


def rel_bucket(rel):
    half = N_BUCKETS // 2
    max_exact = half // 2
    ret = jnp.where(rel > 0, half, 0)
    n = jnp.abs(rel)
    nf = jnp.maximum(n, 1).astype(jnp.float32)
    large = max_exact + (jnp.log(nf / max_exact) / math.log(MAX_DIST / max_exact)
                         * (half - max_exact)).astype(jnp.int32)
    large = jnp.minimum(large, half - 1)
    return ret + jnp.where(n < max_exact, n, large)


def rel_bias(table, q_pos, k_pos):
    bucket = rel_bucket(k_pos[None, :] - q_pos[:, None])
    return jnp.moveaxis(table[bucket].astype(jnp.float32), -1, 0)


def project_inputs(x, norm1_g, w_in, q_norm_g, k_norm_g, gm_norm_g):
    B, T, _ = x.shape
    h = rms_norm(x, norm1_g)
    q, k, v, gu, gv, ga, gg = jnp.split(h @ w_in, SPLITS, axis=-1)
    q = rms_norm(q.reshape(B, T, N_HEADS, 2, DK), q_norm_g)
    k = rms_norm(k.reshape(B, T, N_HEADS, 2, DK), k_norm_g)
    v = v.reshape(B, T, N_HEADS, HEAD_DIM)
    gu = jax.nn.gelu(gu)
    gv = rms_norm(jax.nn.gelu(gv), gm_norm_g)
    return q, k, v, gu, gv, ga, gg


def diff_lambda(lq1, lk1, lq2, lk2, lam_init):
    f = lambda a, b: jnp.exp(jnp.sum(a.astype(jnp.float32) * b.astype(jnp.float32)))
    return f(lq1, lk1) - f(lq2, lk2) + lam_init


def diff_attend(q, k, v, q_pos, k_pos, rel_table, lam, subln_g, lam_init):
    B, Tq = q.shape[:2]
    mask = (k_pos[None, :] // CHUNK) <= (q_pos[:, None] // CHUNK)
    bias = rel_bias(rel_table, q_pos, k_pos)
    s = jnp.einsum('bqhmd,bkhmd->bhmqk', q, k).astype(jnp.float32) * SCALE + bias[None, :, None]
    p = jax.nn.softmax(jnp.where(mask, s, NEG), axis=-1)
    a = p[:, :, 0] - lam * p[:, :, 1]
    o = jnp.einsum('bhqk,bkhd->bqhd', a.astype(v.dtype), v)
    o = rms_norm(o, subln_g) * (1.0 - lam_init)
    return o.reshape(B, Tq, N_HEADS * HEAD_DIM)


def prompt_attention(q, k, v, rel_table, lam, subln_g, lam_init):
    B, S = q.shape[:2]
    nb = S // Q_BLOCK
    qb = q.reshape(B, nb, Q_BLOCK, N_HEADS, 2, DK).swapaxes(0, 1)
    k_pos = jnp.arange(S)

    def one(args):
        q_blk, i = args
        q_pos = i * Q_BLOCK + jnp.arange(Q_BLOCK)
        return diff_attend(q_blk, k, v, q_pos, k_pos, rel_table, lam, subln_g, lam_init)

    o = lax.map(one, (qb, jnp.arange(nb)))
    return o.swapaxes(0, 1).reshape(B, S, V_W)


def gmlp_prompt(gu, gv, w_s, b):
    B, S, _ = gv.shape
    nc = S // GM_CHUNK
    vv = gv.reshape(B, nc, GM_CHUNK, GM_GROUPS, GM_GROUP_DIM)
    wm = (w_s * jnp.tril(jnp.ones((GM_CHUNK, GM_CHUNK), w_s.dtype))).astype(vv.dtype)
    z = jnp.einsum('gts,bnsgc->bntgc', wm, vv) + b.T[None, None, :, :, None].astype(vv.dtype)
    return gu * z.reshape(B, S, GM_WIDTH)


def gmlp_sample(gu, gv, w_s, b):
    B, T, _ = gv.shape
    wm = (w_s * jnp.tril(jnp.ones((GM_CHUNK, GM_CHUNK), w_s.dtype)))[:, :T, :T].astype(gv.dtype)
    z = jnp.einsum('gts,bsgc->btgc', wm, gv.reshape(B, T, GM_GROUPS, GM_GROUP_DIM))
    z = z + b[:, :T].T[None, :, :, None].astype(gv.dtype)
    return gu * z.reshape(B, T, GM_WIDTH)


def merge(attn, gm, ga, gg, w_ab, w_gb, w_out):
    y = jax.nn.sigmoid(ga) * (attn @ w_ab) + jax.nn.sigmoid(gg) * (gm @ w_gb)
    return y @ w_out


def swiglu_ffn(x, norm2_g, w_ffn_in, w_ffn_out):
    g, u = jnp.split(rms_norm(x, norm2_g) @ w_ffn_in, 2, axis=-1)
    return (jax.nn.silu(g) * u) @ w_ffn_out


def setup_inputs(seed: int = 0) -> dict:
    key = jax.random.key(seed)
    ks = jax.random.split(key, 24)
    nrm = lambda k, shape, s: jax.random.normal(k, shape, jnp.float32) * s
    gain = lambda k, shape: 1.0 + 0.02 * jax.random.normal(k, shape, jnp.float32)
    return {
        "x_prompt": nrm(ks[0], (BATCH, SEQ, D_MODEL), 1.0),
        "x_sample": nrm(ks[1], (DEC_BATCH, DEC_SEQ, D_MODEL), 1.0),
        "cache_k": nrm(ks[2], (DEPTH, DEC_BATCH, PAST_LEN, N_HEADS, 2, DK), 1.0),
        "cache_v": nrm(ks[3], (DEPTH, DEC_BATCH, PAST_LEN, N_HEADS, HEAD_DIM), 1.0),
        "rel_table": nrm(ks[4], (N_BUCKETS, N_HEADS), 0.5),
        "norm1_g": gain(ks[5], (DEPTH, D_MODEL)),
        "w_in": nrm(ks[6], (DEPTH, D_MODEL, IN_W), D_MODEL ** -0.5),
        "q_norm_g": gain(ks[7], (DEPTH, DK)),
        "k_norm_g": gain(ks[8], (DEPTH, DK)),
        "lambda_q1": nrm(ks[9], (DEPTH, DK), 0.1),
        "lambda_k1": nrm(ks[10], (DEPTH, DK), 0.1),
        "lambda_q2": nrm(ks[11], (DEPTH, DK), 0.1),
        "lambda_k2": nrm(ks[12], (DEPTH, DK), 0.1),
        "subln_g": gain(ks[13], (DEPTH, HEAD_DIM)),
        "gm_norm_g": gain(ks[14], (DEPTH, GM_WIDTH)),
        "gm_w_s": nrm(ks[15], (DEPTH, GM_GROUPS, GM_CHUNK, GM_CHUNK), GM_CHUNK ** -0.5),
        "gm_b": gain(ks[16], (DEPTH, GM_GROUPS, GM_CHUNK)),
        "w_attn_branch": nrm(ks[17], (DEPTH, V_W, D_MODEL), V_W ** -0.5),
        "w_gmlp_branch": nrm(ks[18], (DEPTH, GM_WIDTH, D_MODEL), GM_WIDTH ** -0.5),
        "w_out": nrm(ks[19], (DEPTH, D_MODEL, D_MODEL), D_MODEL ** -0.5),
        "norm2_g": gain(ks[20], (DEPTH, D_MODEL)),
        "w_ffn_in": nrm(ks[21], (DEPTH, D_MODEL, 2 * D_FF), D_MODEL ** -0.5),
        "w_ffn_out": nrm(ks[22], (DEPTH, D_FF, D_MODEL), D_FF ** -0.5),
    }


def reference(x_prompt, x_sample, cache_k, cache_v, rel_table, norm1_g, w_in, q_norm_g,
              k_norm_g, lambda_q1, lambda_k1, lambda_q2, lambda_k2, subln_g, gm_norm_g,
              gm_w_s, gm_b, w_attn_branch, w_gmlp_branch, w_out, norm2_g, w_ffn_in,
              w_ffn_out):
    xp, xs = x_prompt, x_sample
    past = cache_k.shape[2]
    T = xs.shape[1]
    kp_l, vp_l, ks_l, vs_l, gms_l = [], [], [], [], []
    for l in range(DEPTH):
        lam_init = 0.8 - 0.6 * math.exp(-0.3 * l)
        lam = diff_lambda(lambda_q1[l], lambda_k1[l], lambda_q2[l], lambda_k2[l], lam_init)

        q, k, v, gu, gv, ga, gg = project_inputs(xp, norm1_g[l], w_in[l], q_norm_g[l],
                                                 k_norm_g[l], gm_norm_g[l])
        att = prompt_attention(q, k, v, rel_table, lam, subln_g[l], lam_init)
        gm = gmlp_prompt(gu, gv, gm_w_s[l], gm_b[l])
        xp = xp + merge(att, gm, ga, gg, w_attn_branch[l], w_gmlp_branch[l], w_out[l])
        xp = xp + swiglu_ffn(xp, norm2_g[l], w_ffn_in[l], w_ffn_out[l])
        kp_l.append(k)
        vp_l.append(v)

        q, k, v, gu, gv, ga, gg = project_inputs(xs, norm1_g[l], w_in[l], q_norm_g[l],
                                                 k_norm_g[l], gm_norm_g[l])
        k_all = jnp.concatenate([cache_k[l].astype(k.dtype), k], axis=1)
        v_all = jnp.concatenate([cache_v[l].astype(v.dtype), v], axis=1)
        q_pos = past + jnp.arange(T)
        k_pos = jnp.arange(past + T)
        att = diff_attend(q, k_all, v_all, q_pos, k_pos, rel_table, lam, subln_g[l], lam_init)
        gm = gmlp_sample(gu, gv, gm_w_s[l], gm_b[l])
        xs = xs + merge(att, gm, ga, gg, w_attn_branch[l], w_gmlp_branch[l], w_out[l])
        xs = xs + swiglu_ffn(xs, norm2_g[l], w_ffn_in[l], w_ffn_out[l])
        ks_l.append(k)
        vs_l.append(v)
        gms_l.append(gv)

    new_k_prompt = jnp.stack(kp_l)
    new_v_prompt = jnp.stack(vp_l)
    new_k_sample = jnp.stack(ks_l)
    new_v_sample = jnp.stack(vs_l)
    new_gm_v_sample = jnp.stack(gms_l)
    return (xp, xs, new_k_prompt, new_v_prompt, new_k_sample, new_v_sample, new_gm_v_sample)
```

```python
import functools
import math

import jax
import jax.numpy as jnp
from jax import lax
from jax.experimental import pallas as pl
from jax.experimental.pallas import tpu as pltpu

F32 = jnp.float32
BF16 = jnp.bfloat16

D_MODEL = 1024
N_HEADS = 8
HEAD_DIM = D_MODEL // N_HEADS
DK = HEAD_DIM // 2
SCALE = DK ** -0.5
CHUNK = 64
N_BUCKETS = 32
MAX_DIST = 128
GM_CHUNK = 128
GM_GROUPS = 8
D_FF = 2816
N_SECTIONS = 7
EPS = 1e-6
NEG = -1e30

LANES = 128
VMEM_LIMIT_BYTES = 56 * 1024 * 1024

HALF_BUCKETS = N_BUCKETS // 2
MAX_EXACT = HALF_BUCKETS // 2
LOG_THRESHOLDS = tuple(
    math.ceil(MAX_EXACT * (MAX_DIST / MAX_EXACT) ** (k / (HALF_BUCKETS - MAX_EXACT)) - 1e-9)
    for k in range(1, HALF_BUCKETS - MAX_EXACT))
FAR_BUCKET = HALF_BUCKETS - 1

TQ = 128
KT = 128
STRIP_W = TQ + KT
assert LOG_THRESHOLDS[-1] <= KT + 1 and TQ % CHUNK == 0


def _rms(x, g):
    return x * lax.rsqrt(jnp.mean(x * x, axis=-1, keepdims=True) + EPS) * g


def _const_spec(shape):
    nd = len(shape)
    return pl.BlockSpec(shape, lambda *_: (0,) * nd, pipeline_mode=pl.Buffered(1))


def _bias_kernel(tab_ref, lq1_ref, lk1_ref, lq2_ref, lk2_ref, strip_ref, lam_ref, *, lam_init):
    h = pl.program_id(0)
    i = lax.broadcasted_iota(jnp.int32, (TQ, STRIP_W), 0)
    j = lax.broadcasted_iota(jnp.int32, (TQ, STRIP_W), 1)
    rel = j - KT - i
    n = jnp.abs(rel)
    large = jnp.full_like(n, MAX_EXACT)
    for t in LOG_THRESHOLDS:
        large = large + (n >= t).astype(jnp.int32)
    bucket = jnp.where(n < MAX_EXACT, n, large) + jnp.where(rel > 0, HALF_BUCKETS, 0)
    bias = jnp.zeros((TQ, STRIP_W), F32)
    for b in range(N_BUCKETS):
        bias = jnp.where(bucket == b, tab_ref[b, h], bias)
    jd = j - KT
    masked = (j >= KT) & (jnp.right_shift(jd, 6) > jnp.right_shift(i, 6))
    strip_ref[...] = jnp.where(masked, NEG, bias)

    e1 = jnp.exp(jnp.sum(lq1_ref[...] * lk1_ref[...], axis=-1, keepdims=True))
    e2 = jnp.exp(jnp.sum(lq2_ref[...] * lk2_ref[...], axis=-1, keepdims=True))
    lam_ref[...] = jnp.broadcast_to(e1 - e2 + lam_init, lam_ref.shape)


def _bias_and_lambda(rel_table, lq1, lk1, lq2, lk2, lam_init):
    vec = pl.BlockSpec((1, DK), lambda h: (0, 0))
    strip, lam = pl.pallas_call(
        functools.partial(_bias_kernel, lam_init=lam_init),
        grid=(N_HEADS,),
        in_specs=[pl.BlockSpec(memory_space=pltpu.SMEM), vec, vec, vec, vec],
        out_specs=[pl.BlockSpec((None, TQ, STRIP_W), lambda h: (h, 0, 0)),
                   pl.BlockSpec((8, LANES), lambda h: (0, 0))],
        out_shape=[jax.ShapeDtypeStruct((N_HEADS, TQ, STRIP_W), F32),
                   jax.ShapeDtypeStruct((8, LANES), F32)],
        compiler_params=pltpu.CompilerParams(dimension_semantics=("arbitrary",)),
        name="bias",
    )(rel_table, lq1.reshape(1, DK), lk1.reshape(1, DK), lq2.reshape(1, DK), lk2.reshape(1, DK))
    return strip, lam[0:1, 0:1]


def _proj_kernel(x_ref, g1_ref, w_ref, qg_ref, kg_ref, gmg_ref, gsum_ref, gbc_ref,
                 q_ref, kf_ref, kb_ref, vf_ref, vb_ref, gu_ref, gv_ref, sa_ref, sg_ref, *gvf_ref):
    h = _rms(x_ref[...], g1_ref[...]).astype(BF16)

    def section(i):
        return jnp.dot(h, w_ref[:, i * D_MODEL:(i + 1) * D_MODEL], preferred_element_type=F32)

    def group_norm(y, g):
        ss = jnp.dot((y * y).astype(BF16), gsum_ref[...], preferred_element_type=F32)
        inv = lax.rsqrt(ss * (1.0 / DK) + EPS)
        hi = inv.astype(BF16)
        lo = (inv - hi.astype(F32)).astype(BF16)
        inv_b = jnp.dot(jnp.concatenate([hi, lo], axis=-1), gbc_ref[...], preferred_element_type=F32)
        return y * inv_b * g

    q_ref[...] = (group_norm(section(0), qg_ref[...]) * SCALE).astype(BF16)
    k = group_norm(section(1), kg_ref[...])
    kf_ref[...] = k
    kb_ref[...] = k.astype(BF16)
    v = section(2)
    vf_ref[...] = v
    vb_ref[...] = v.astype(BF16)
    gu_ref[...] = jax.nn.gelu(section(3)).astype(BF16)
    gv = _rms(jax.nn.gelu(section(4)), gmg_ref[...])
    gv_ref[...] = gv.astype(BF16)
    if gvf_ref:
        gvf_ref[0][...] = gv
    sa_ref[...] = jax.nn.sigmoid(section(5)).astype(BF16)
    sg_ref[...] = jax.nn.sigmoid(section(6)).astype(BF16)


def _project(x2d, g1, w_in, qg, kg, gmg, gsum, gbc, *, tm, emit_gv_f32):
    n = x2d.shape[0]
    row = pl.BlockSpec((tm, D_MODEL), lambda i: (i, 0))
    out_dtypes = [BF16, F32, BF16, F32, BF16, BF16, BF16, BF16, BF16] + ([F32] if emit_gv_f32 else [])
    return pl.pallas_call(
        _proj_kernel,
        grid=(n // tm,),
        in_specs=[row, _const_spec((1, D_MODEL)), _const_spec(w_in.shape), _const_spec((1, D_MODEL)),
                  _const_spec((1, D_MODEL)), _const_spec((1, D_MODEL)), _const_spec(gsum.shape),
                  _const_spec(gbc.shape)],
        out_specs=[row] * len(out_dtypes),
        out_shape=[jax.ShapeDtypeStruct((n, D_MODEL), dt) for dt in out_dtypes],
        compiler_params=pltpu.CompilerParams(dimension_semantics=("arbitrary",),
                                             vmem_limit_bytes=VMEM_LIMIT_BYTES),
        name="proj",
    )(x2d, g1, w_in, qg, kg, gmg, gsum, gbc)


def _split_maps(q):
    lane = lax.broadcasted_iota(jnp.int32, q.shape, 1)
    zero = jnp.zeros_like(q)
    return jnp.concatenate([jnp.where(lane < DK, q, zero), jnp.where(lane >= DK, q, zero)], axis=0)


def _scores(qq, k):
    return lax.dot_general(qq, k, (((1,), (1,)), ((), ())), preferred_element_type=F32)


def _finish_heads(o, l, lam, sg, lam_init):
    r = o.shape[0] // 2
    d = o[:r] / l[:r] - lam * (o[r:] / l[r:])
    return _rms(d, sg) * (1.0 - lam_init)


def _attn_prompt_kernel(tab_ref, lam_ref, q_ref, k_ref, v_ref, strip_ref, sg_ref, o_ref, s_scr,
                        *, seq, lam_init):
    h = pl.program_id(1)
    c_far = tab_ref[FAR_BUCKET, h]
    lam = lam_ref[0, 0]
    near_col = seq
    strip = jnp.concatenate([strip_ref[...], strip_ref[...]], axis=0)

    def q_block(qi, carry):
        q0 = pl.multiple_of(qi * TQ, TQ)
        qq = _split_maps(q_ref[pl.ds(q0, TQ), :])
        n_far = jnp.maximum(qi - 1, 0)
        near0 = pl.multiple_of(jnp.maximum(q0 - KT, 0), KT)
        near_pen = jnp.where(qi == 0, NEG, 0.0)

        def far_scores(kt, m):
            k0 = pl.multiple_of(kt * KT, KT)
            s = _scores(qq, k_ref[pl.ds(k0, KT), :]) + c_far
            s_scr[:, pl.ds(k0, KT)] = s
            return jnp.maximum(m, s)

        m = lax.fori_loop(0, n_far, far_scores, jnp.full((2 * TQ, KT), -jnp.inf, F32))
        s_near = _scores(qq, k_ref[pl.ds(near0, KT), :]) + (strip[:, :KT] + near_pen)
        s_diag = _scores(qq, k_ref[pl.ds(q0, TQ), :]) + strip[:, KT:]
        s_scr[:, near_col:near_col + KT] = s_near
        s_scr[:, near_col + KT:near_col + KT + TQ] = s_diag
        m = jnp.maximum(jnp.maximum(m, s_near), s_diag)
        m = jnp.max(m, axis=-1, keepdims=True)

        def far_values(kt, lo):
            l, o = lo
            k0 = pl.multiple_of(kt * KT, KT)
            e = jnp.exp(s_scr[:, pl.ds(k0, KT)] - m)
            return l + e, o + jnp.dot(e.astype(BF16), v_ref[pl.ds(k0, KT), :], preferred_element_type=F32)

        zeros = jnp.zeros((2 * TQ, KT), F32)
        l, o = lax.fori_loop(0, n_far, far_values, (zeros, zeros))
        e_near = jnp.exp(s_scr[:, near_col:near_col + KT] - m)
        e_diag = jnp.exp(s_scr[:, near_col + KT:near_col + KT + TQ] - m)
        o = o + jnp.dot(e_near.astype(BF16), v_ref[pl.ds(near0, KT), :], preferred_element_type=F32)
        o = o + jnp.dot(e_diag.astype(BF16), v_ref[pl.ds(q0, TQ), :], preferred_element_type=F32)
        l = jnp.sum(l + e_near + e_diag, axis=-1, keepdims=True)
        o_ref[pl.ds(q0, TQ), :] = _finish_heads(o, l, lam, sg_ref[...], lam_init).astype(BF16)
        return carry

    lax.fori_loop(0, seq // TQ, q_block, 0)


def _attend_prompt(rel_table, lam, q, k, v, strip, sg, lam_init):
    batch, seq, _ = q.shape
    head = pl.BlockSpec((None, seq, HEAD_DIM), lambda b, h: (b, 0, h))
    return pl.pallas_call(
        functools.partial(_attn_prompt_kernel, seq=seq, lam_init=lam_init),
        grid=(batch, N_HEADS),
        in_specs=[pl.BlockSpec(memory_space=pltpu.SMEM), pl.BlockSpec(memory_space=pltpu.SMEM),
                  head, head, head,
                  pl.BlockSpec((None, TQ, STRIP_W), lambda b, h: (h, 0, 0)),
                  pl.BlockSpec((1, HEAD_DIM), lambda b, h: (0, 0))],
        out_specs=head,
        out_shape=jax.ShapeDtypeStruct((batch, seq, D_MODEL), BF16),
        scratch_shapes=[pltpu.VMEM((2 * TQ, seq + STRIP_W), F32)],
        compiler_params=pltpu.CompilerParams(dimension_semantics=("arbitrary", "arbitrary")),
        name="attn_prompt",
    )(rel_table, lam, q, k, v, strip, sg)


def _attn_sample_kernel(tab_ref, lam_ref, q_ref, ck_ref, cv_ref, kn_ref, vn_ref, strip_ref, sg_ref,
                        o_ref, *, past, t_new, lam_init):
    h = pl.program_id(1)
    c_far = tab_ref[FAR_BUCKET, h]
    lam = lam_ref[0, 0]
    far = past - KT
    strip = jnp.concatenate([strip_ref[:t_new, :], strip_ref[:t_new, :]], axis=0)
    qq = _split_maps(q_ref[...])
    s_c = _scores(qq, ck_ref[...].astype(BF16))
    s_far = s_c[:, :far] + c_far
    s_near = s_c[:, far:] + strip[:, :KT]
    s_new = _scores(qq, kn_ref[...]) + strip[:, KT:KT + t_new]
    m = jnp.maximum(jnp.maximum(jnp.max(s_far, axis=-1, keepdims=True),
                                jnp.max(s_near, axis=-1, keepdims=True)),
                    jnp.max(s_new, axis=-1, keepdims=True))
    e_far = jnp.exp(s_far - m)
    e_near = jnp.exp(s_near - m)
    e_new = jnp.exp(s_new - m)
    l = (jnp.sum(e_far, axis=-1, keepdims=True) + jnp.sum(e_near, axis=-1, keepdims=True)
         + jnp.sum(e_new, axis=-1, keepdims=True))
    cv = cv_ref[...].astype(BF16)
    o = (jnp.dot(e_far.astype(BF16), cv[:far], preferred_element_type=F32)
         + jnp.dot(e_near.astype(BF16), cv[far:], preferred_element_type=F32)
         + jnp.dot(e_new.astype(BF16), vn_ref[...], preferred_element_type=F32))
    o_ref[...] = _finish_heads(o, l, lam, sg_ref[...], lam_init).astype(BF16)


def _attend_sample(rel_table, lam, q, cache_k, cache_v, k_new, v_new, strip, sg, lam_init):
    batch, t_new, _ = q.shape
    past = cache_k.shape[1]
    assert past % CHUNK == 0 and t_new <= CHUNK and past > KT
    new = pl.BlockSpec((None, t_new, HEAD_DIM), lambda b, h: (b, 0, h))
    cached = pl.BlockSpec((None, past, HEAD_DIM), lambda b, h: (b, 0, h))
    return pl.pallas_call(
        functools.partial(_attn_sample_kernel, past=past, t_new=t_new, lam_init=lam_init),
        grid=(batch, N_HEADS),
        in_specs=[pl.BlockSpec(memory_space=pltpu.SMEM), pl.BlockSpec(memory_space=pltpu.SMEM),
                  new, cached, cached, new, new,
                  pl.BlockSpec((None, TQ, STRIP_W), lambda b, h: (h, 0, 0)),
                  pl.BlockSpec((1, HEAD_DIM), lambda b, h: (0, 0))],
        out_specs=new,
        out_shape=jax.ShapeDtypeStruct((batch, t_new, D_MODEL), BF16),
        compiler_params=pltpu.CompilerParams(dimension_semantics=("arbitrary", "arbitrary")),
        name="attn_sample",
    )(rel_table, lam, q, cache_k, cache_v, k_new, v_new, strip, sg)


def _merge_ffn_kernel(x_ref, att_ref, gu_ref, gv_ref, sa_ref, sg_ref, ws_ref, bt_ref, wab_ref,
                      wgb_ref, wout_ref, g2_ref, wfi_ref, wfo_ref, o_ref, gm_scr, *, tm, tc):
    row = lax.broadcasted_iota(jnp.int32, (GM_CHUNK, GM_CHUNK), 0)
    col = lax.broadcasted_iota(jnp.int32, (GM_CHUNK, GM_CHUNK), 1)
    for g in range(GM_GROUPS):
        cols = slice(g * GM_CHUNK, (g + 1) * GM_CHUNK)
        wm = jnp.where(row >= col, ws_ref[g], 0.0).astype(BF16)[:tc]
        b_col = bt_ref[:tc, g:g + 1]
        for c in range(tm // tc):
            rows = slice(c * tc, (c + 1) * tc)
            vv = gv_ref[rows, cols]
            if tc < GM_CHUNK:
                vv = jnp.concatenate([vv, jnp.zeros((GM_CHUNK - tc, GM_CHUNK), BF16)], axis=0)
            z = jnp.dot(wm, vv, preferred_element_type=F32) + b_col
            gm_scr[rows, cols] = (gu_ref[rows, cols].astype(F32) * z).astype(BF16)

    y = (sa_ref[...].astype(F32) * jnp.dot(att_ref[...], wab_ref[...], preferred_element_type=F32)
         + sg_ref[...].astype(F32) * jnp.dot(gm_scr[...], wgb_ref[...], preferred_element_type=F32))
    x1 = x_ref[...] + jnp.dot(y.astype(BF16), wout_ref[...], preferred_element_type=F32)
    h2 = _rms(x1, g2_ref[...]).astype(BF16)
    gate = jnp.dot(h2, wfi_ref[:, :D_FF], preferred_element_type=F32)
    up = jnp.dot(h2, wfi_ref[:, D_FF:], preferred_element_type=F32)
    act = (jax.nn.silu(gate) * up).astype(BF16)
    o_ref[...] = x1 + jnp.dot(act, wfo_ref[...], preferred_element_type=F32)


def _merge_ffn(x2d, att, gu, gv, sa, sg, w_s, b_t, w_ab, w_gb, w_out, g2, w_fi, w_fo, *, tm, tc):
    n = x2d.shape[0]
    row = pl.BlockSpec((tm, D_MODEL), lambda i: (i, 0))
    consts = [w_s, b_t, w_ab, w_gb, w_out, g2, w_fi, w_fo]
    return pl.pallas_call(
        functools.partial(_merge_ffn_kernel, tm=tm, tc=tc),
        grid=(n // tm,),
        in_specs=[row] * 6 + [_const_spec(c.shape) for c in consts],
        out_specs=row,
        out_shape=jax.ShapeDtypeStruct((n, D_MODEL), F32),
        scratch_shapes=[pltpu.VMEM((tm, D_MODEL), BF16)],
        compiler_params=pltpu.CompilerParams(dimension_semantics=("arbitrary",),
                                             vmem_limit_bytes=VMEM_LIMIT_BYTES),
        name="merge_ffn",
    )(x2d, att, gu, gv, sa, sg, *consts)


def kernel(x_prompt, x_sample, cache_k, cache_v, rel_table, norm1_g, w_in, q_norm_g, k_norm_g,
           lambda_q1, lambda_k1, lambda_q2, lambda_k2, subln_g, gm_norm_g, gm_w_s, gm_b,
           w_attn_branch, w_gmlp_branch, w_out, norm2_g, w_ffn_in, w_ffn_out):
    depth = w_in.shape[0]
    assert depth == 1
    batch, seq, _ = x_prompt.shape
    dec_batch, t_new, _ = x_sample.shape
    past = cache_k.shape[2]
    lam_init = 0.8 - 0.6 * math.exp(-0.3 * 0)

    w_in_b = w_in[0].astype(BF16)
    w_ab = w_attn_branch[0].astype(BF16)
    w_gb = w_gmlp_branch[0].astype(BF16)
    w_o = w_out[0].astype(BF16)
    w_fi = w_ffn_in[0].astype(BF16)
    w_fo = w_ffn_out[0].astype(BF16)
    g1 = norm1_g[0].reshape(1, D_MODEL)
    g2 = norm2_g[0].reshape(1, D_MODEL)
    gmg = gm_norm_g[0].reshape(1, D_MODEL)
    qg = jnp.tile(q_norm_g[0], 2 * N_HEADS).reshape(1, D_MODEL)
    kg = jnp.tile(k_norm_g[0], 2 * N_HEADS).reshape(1, D_MODEL)
    sg = subln_g[0].reshape(1, HEAD_DIM)
    b_t = gm_b[0].T
    group = jnp.arange(D_MODEL) // DK
    gsum = (group[:, None] == jnp.arange(LANES)[None, :]).astype(BF16)
    gbc = jnp.concatenate([gsum.T, gsum.T], axis=0)

    strip, lam = _bias_and_lambda(rel_table, lambda_q1[0], lambda_k1[0], lambda_q2[0], lambda_k2[0],
                                  lam_init)

    xp = x_prompt.reshape(batch * seq, D_MODEL)
    q, kf, kb, vf, vb, gu, gv, sa, sgt = _project(xp, g1, w_in_b, qg, kg, gmg, gsum, gbc,
                                                  tm=256, emit_gv_f32=False)
    shp = (batch, seq, D_MODEL)
    att = _attend_prompt(rel_table, lam, q.reshape(shp), kb.reshape(shp), vb.reshape(shp), strip, sg,
                         lam_init)
    yp = _merge_ffn(xp, att.reshape(batch * seq, D_MODEL), gu, gv, sa, sgt, gm_w_s[0], b_t, w_ab, w_gb,
                    w_o, g2, w_fi, w_fo, tm=256, tc=GM_CHUNK)

    xs = x_sample.reshape(dec_batch * t_new, D_MODEL)
    q_s, kf_s, kb_s, vf_s, vb_s, gu_s, gv_s, sa_s, sg_s, gvf_s = _project(
        xs, g1, w_in_b, qg, kg, gmg, gsum, gbc, tm=256, emit_gv_f32=True)
    shs = (dec_batch, t_new, D_MODEL)
    att_s = _attend_sample(rel_table, lam, q_s.reshape(shs),
                           cache_k[0].reshape(dec_batch, past, D_MODEL),
                           cache_v[0].reshape(dec_batch, past, D_MODEL),
                           kb_s.reshape(shs), vb_s.reshape(shs), strip, sg, lam_init)
    ys = _merge_ffn(xs, att_s.reshape(dec_batch * t_new, D_MODEL), gu_s, gv_s, sa_s, sg_s, gm_w_s[0],
                    b_t, w_ab, w_gb, w_o, g2, w_fi, w_fo, tm=256, tc=t_new)

    return (yp.reshape(batch, seq, D_MODEL),
            ys.reshape(dec_batch, t_new, D_MODEL),
            kf.reshape(1, batch, seq, N_HEADS, 2, DK),
            vf.reshape(1, batch, seq, N_HEADS, HEAD_DIM),
            kf_s.reshape(1, dec_batch, t_new, N_HEADS, 2, DK),
            vf_s.reshape(1, dec_batch, t_new, N_HEADS, HEAD_DIM),
            gvf_s.reshape(1, dec_batch, t_new, D_MODEL))
```

```python
import functools
import math

import jax
import jax.numpy as jnp
from jax import lax
from jax.experimental import pallas as pl
from jax.experimental.pallas import tpu as pltpu

F32 = jnp.float32
BF16 = jnp.bfloat16

D_MODEL = 1024
N_HEADS = 8
HEAD_DIM = D_MODEL // N_HEADS
DK = HEAD_DIM // 2
SCALE = DK ** -0.5
CHUNK = 64
N_BUCKETS = 32
MAX_DIST = 128
GM_CHUNK = 128
GM_GROUPS = 8
D_FF = 2816
EPS = 1e-6
NEG = -1e30

LANES = 128
SUBLANES = 8
VMEM_LIMIT_BYTES = 56 * 1024 * 1024

HALF_BUCKETS = N_BUCKETS // 2
MAX_EXACT = HALF_BUCKETS // 2
LOG_THRESHOLDS = tuple(
    math.ceil(MAX_EXACT * (MAX_DIST / MAX_EXACT) ** (k / (HALF_BUCKETS - MAX_EXACT)) - 1e-9)
    for k in range(1, HALF_BUCKETS - MAX_EXACT))
FAR_BUCKET = HALF_BUCKETS - 1

TQ = 256
NEAR = 128
STRIP_W = NEAR + TQ
FAR_TILE = 512
PROJ_ROWS = 256
assert LOG_THRESHOLDS[-1] <= NEAR + 1 and TQ % CHUNK == 0 and NEAR % LANES == 0


def _rms(x, g):
    return x * lax.rsqrt(jnp.mean(x * x, axis=-1, keepdims=True) + EPS) * g


def _const_spec(shape):
    nd = len(shape)
    return pl.BlockSpec(shape, lambda *_: (0,) * nd, pipeline_mode=pl.Buffered(1))


def _bias_kernel(tab_ref, lq1_ref, lk1_ref, lq2_ref, lk2_ref, strip_ref, lam_ref, *, lam_init):
    h = pl.program_id(0)
    i = lax.broadcasted_iota(jnp.int32, (TQ, STRIP_W), 0)
    j = lax.broadcasted_iota(jnp.int32, (TQ, STRIP_W), 1)
    rel = j - NEAR - i
    n = jnp.abs(rel)
    large = jnp.full_like(n, MAX_EXACT)
    for t in LOG_THRESHOLDS:
        large = large + (n >= t).astype(jnp.int32)
    bucket = jnp.where(n < MAX_EXACT, n, large) + jnp.where(rel > 0, HALF_BUCKETS, 0)
    bias = jnp.zeros((TQ, STRIP_W), F32)
    for b in range(N_BUCKETS):
        bias = jnp.where(bucket == b, tab_ref[b, h], bias)
    jd = j - NEAR
    masked = (j >= NEAR) & (jnp.right_shift(jd, 6) > jnp.right_shift(i, 6))
    strip_ref[...] = jnp.where(masked, NEG, bias)

    e1 = jnp.exp(jnp.sum(lq1_ref[...] * lk1_ref[...], axis=-1, keepdims=True))
    e2 = jnp.exp(jnp.sum(lq2_ref[...] * lk2_ref[...], axis=-1, keepdims=True))
    lam_ref[...] = jnp.broadcast_to(e1 - e2 + lam_init, lam_ref.shape)


def _bias_and_lambda(rel_table, lq1, lk1, lq2, lk2, lam_init):
    vec = pl.BlockSpec((1, DK), lambda h: (0, 0))
    strip, lam = pl.pallas_call(
        functools.partial(_bias_kernel, lam_init=lam_init),
        grid=(N_HEADS,),
        in_specs=[pl.BlockSpec(memory_space=pltpu.SMEM), vec, vec, vec, vec],
        out_specs=[pl.BlockSpec((None, TQ, STRIP_W), lambda h: (h, 0, 0)),
                   pl.BlockSpec((SUBLANES, LANES), lambda h: (0, 0))],
        out_shape=[jax.ShapeDtypeStruct((N_HEADS, TQ, STRIP_W), F32),
                   jax.ShapeDtypeStruct((SUBLANES, LANES), F32)],
        compiler_params=pltpu.CompilerParams(dimension_semantics=("arbitrary",)),
        name="bias",
    )(rel_table, lq1.reshape(1, DK), lk1.reshape(1, DK), lq2.reshape(1, DK), lk2.reshape(1, DK))
    return strip, lam[0:1, 0:1]


def _proj_kernel(x_ref, g1_ref, w_ref, qg_ref, kg_ref, gmg_ref, gsum_ref, gbc_ref,
                 q_ref, kf_ref, kb_ref, vf_ref, vb_ref, gu_ref, gv_ref, sa_ref, sg_ref, *gvf_ref,
                 transpose_k):
    h = _rms(x_ref[...], g1_ref[...]).astype(BF16)

    def section(i):
        return jnp.dot(h, w_ref[:, i * D_MODEL:(i + 1) * D_MODEL], preferred_element_type=F32)

    def group_norm(y, g):
        ss = jnp.dot((y * y).astype(BF16), gsum_ref[...], preferred_element_type=F32)
        inv = lax.rsqrt(ss * (1.0 / DK) + EPS)
        hi = inv.astype(BF16)
        lo = (inv - hi.astype(F32)).astype(BF16)
        inv_b = jnp.dot(jnp.concatenate([hi, lo], axis=-1), gbc_ref[...], preferred_element_type=F32)
        return y * inv_b * g

    q_ref[...] = (group_norm(section(0), qg_ref[...]) * SCALE).astype(BF16)
    k = group_norm(section(1), kg_ref[...])
    if transpose_k:
        k = k.T
    kf_ref[...] = k
    kb_ref[...] = k.astype(BF16)
    v = section(2)
    vf_ref[...] = v
    vb_ref[...] = v.astype(BF16)
    gu_ref[...] = jax.nn.gelu(section(3)).astype(BF16)
    gv = _rms(jax.nn.gelu(section(4)), gmg_ref[...])
    gv_ref[...] = gv.astype(BF16)
    if gvf_ref:
        gvf_ref[0][...] = gv
    sa_ref[...] = jax.nn.sigmoid(section(5)).astype(BF16)
    sg_ref[...] = jax.nn.sigmoid(section(6)).astype(BF16)


def _project(x3d, g1, w_in, qg, kg, gmg, gsum, gbc, *, tm, transpose_k, emit_gv_f32):
    batch, rows, _ = x3d.shape
    steps = rows // tm
    n = batch * rows
    row = pl.BlockSpec((tm, D_MODEL), lambda b, i: (b * steps + i, 0))
    row_shape = lambda dt: jax.ShapeDtypeStruct((n, D_MODEL), dt)
    if transpose_k:
        k_spec = pl.BlockSpec((None, D_MODEL, tm), lambda b, i: (b, 0, i))
        k_shape = lambda dt: jax.ShapeDtypeStruct((batch, D_MODEL, rows), dt)
    else:
        k_spec, k_shape = row, row_shape
    out_specs = [row, k_spec, k_spec] + [row] * (6 + int(emit_gv_f32))
    out_shape = ([row_shape(BF16), k_shape(F32), k_shape(BF16), row_shape(F32)]
                 + [row_shape(BF16)] * 5 + ([row_shape(F32)] if emit_gv_f32 else []))
    return pl.pallas_call(
        functools.partial(_proj_kernel, transpose_k=transpose_k),
        grid=(batch, steps),
        in_specs=[pl.BlockSpec((None, tm, D_MODEL), lambda b, i: (b, i, 0)),
                  _const_spec((1, D_MODEL)), _const_spec(w_in.shape), _const_spec((1, D_MODEL)),
                  _const_spec((1, D_MODEL)), _const_spec((1, D_MODEL)), _const_spec(gsum.shape),
                  _const_spec(gbc.shape)],
        out_specs=out_specs,
        out_shape=out_shape,
        compiler_params=pltpu.CompilerParams(dimension_semantics=("arbitrary", "arbitrary"),
                                             vmem_limit_bytes=VMEM_LIMIT_BYTES),
        name="proj",
    )(x3d, g1, w_in, qg, kg, gmg, gsum, gbc)


def _split_maps(q):
    lane = lax.broadcasted_iota(jnp.int32, q.shape, 1)
    zero = jnp.zeros_like(q)
    return jnp.concatenate([jnp.where(lane < DK, q, zero), jnp.where(lane >= DK, q, zero)], axis=0)


def _both_maps(x):
    return jnp.concatenate([x, x], axis=0)


def _finish_heads(o, l, lam, sg, lam_init):
    r = o.shape[0] // 2
    d = o[:r] / l[:r] - lam * (o[r:] / l[r:])
    return _rms(d, sg) * (1.0 - lam_init)


def _attn_prompt_kernel(tab_ref, lam_ref, q_ref, kt_ref, v_ref, strip_ref, sg_ref, o_ref, s_scr,
                        *, seq, lam_init):
    h = pl.program_id(1)
    c_far = tab_ref[FAR_BUCKET, h]
    lam = lam_ref[0, 0]

    for q0 in range(0, seq, TQ):
        qq = _split_maps(q_ref[q0:q0 + TQ, :])
        far_end = max(q0 - NEAR, 0)
        far_tiles = [(k0, min(FAR_TILE, far_end - k0)) for k0 in range(0, far_end, FAR_TILE)]
        end = q0 + TQ

        m_far = None
        for k0, kw in far_tiles:
            s = jnp.dot(qq, kt_ref[:, k0:k0 + kw], preferred_element_type=F32)
            s_scr[:, k0:k0 + kw] = s
            mt = jnp.max(s, axis=-1, keepdims=True)
            m_far = mt if m_far is None else jnp.maximum(m_far, mt)
        bias = _both_maps(strip_ref[:, STRIP_W - (end - far_end):])
        s = jnp.dot(qq, kt_ref[:, far_end:end], preferred_element_type=F32) + bias
        s_scr[:, far_end:end] = s
        m = jnp.max(s, axis=-1, keepdims=True)
        if m_far is not None:
            m = jnp.maximum(m, m_far + c_far)

        e = jnp.exp(s_scr[:, far_end:end] - m)
        l = jnp.sum(e, axis=-1, keepdims=True)
        o = jnp.dot(e.astype(BF16), v_ref[far_end:end, :], preferred_element_type=F32)
        m_shift = m - c_far
        for k0, kw in far_tiles:
            e = jnp.exp(s_scr[:, k0:k0 + kw] - m_shift)
            l = l + jnp.sum(e, axis=-1, keepdims=True)
            o = o + jnp.dot(e.astype(BF16), v_ref[k0:k0 + kw, :], preferred_element_type=F32)
        o_ref[q0:end, :] = _finish_heads(o, l, lam, sg_ref[...], lam_init).astype(BF16)


def _attend_prompt(rel_table, lam, q, kt, v, strip, sg, lam_init):
    batch, seq, _ = q.shape
    head = pl.BlockSpec((None, seq, HEAD_DIM), lambda b, h: (b, 0, h))
    return pl.pallas_call(
        functools.partial(_attn_prompt_kernel, seq=seq, lam_init=lam_init),
        grid=(batch, N_HEADS),
        in_specs=[pl.BlockSpec(memory_space=pltpu.SMEM), pl.BlockSpec(memory_space=pltpu.SMEM),
                  head,
                  pl.BlockSpec((None, HEAD_DIM, seq), lambda b, h: (b, h, 0)),
                  head,
                  pl.BlockSpec((None, TQ, STRIP_W), lambda b, h: (h, 0, 0)),
                  pl.BlockSpec((1, HEAD_DIM), lambda b, h: (0, 0))],
        out_specs=head,
        out_shape=jax.ShapeDtypeStruct((batch, seq, D_MODEL), BF16),
        scratch_shapes=[pltpu.VMEM((2 * TQ, seq), F32)],
        compiler_params=pltpu.CompilerParams(dimension_semantics=("arbitrary", "arbitrary"),
                                             vmem_limit_bytes=VMEM_LIMIT_BYTES),
        name="attn_prompt",
    )(rel_table, lam, q, kt, v, strip, sg)


def _attn_sample_kernel(tab_ref, lam_ref, q_ref, ckt_ref, cv_ref, kn_ref, vn_ref, strip_ref, sg_ref,
                        o_ref, *, past, t_new, lam_init):
    lam = lam_ref[0, 0]
    far = past - NEAR
    for h in range(N_HEADS):
        cols = slice(h * HEAD_DIM, (h + 1) * HEAD_DIM)
        c_far = tab_ref[FAR_BUCKET, h]
        strip = _both_maps(strip_ref[h, :t_new, :])
        qq = _split_maps(q_ref[:, cols])
        s_c = jnp.dot(qq, ckt_ref[cols, :].astype(BF16), preferred_element_type=F32)
        s_near = s_c[:, far:] + strip[:, :NEAR]
        s_new = lax.dot_general(qq, kn_ref[:, cols], (((1,), (1,)), ((), ())),
                                preferred_element_type=F32) + strip[:, NEAR:NEAR + t_new]
        m = jnp.maximum(jnp.maximum(jnp.max(s_c[:, :far], axis=-1, keepdims=True) + c_far,
                                    jnp.max(s_near, axis=-1, keepdims=True)),
                        jnp.max(s_new, axis=-1, keepdims=True))
        e_far = jnp.exp(s_c[:, :far] - (m - c_far))
        e_near = jnp.exp(s_near - m)
        e_new = jnp.exp(s_new - m)
        l = (jnp.sum(e_far, axis=-1, keepdims=True) + jnp.sum(e_near, axis=-1, keepdims=True)
             + jnp.sum(e_new, axis=-1, keepdims=True))
        cv = cv_ref[pl.ds(h, past, stride=N_HEADS), :].astype(BF16)
        o = (jnp.dot(e_far.astype(BF16), cv[:far], preferred_element_type=F32)
             + jnp.dot(e_near.astype(BF16), cv[far:], preferred_element_type=F32)
             + jnp.dot(e_new.astype(BF16), vn_ref[:, cols], preferred_element_type=F32))
        o_ref[:, cols] = _finish_heads(o, l, lam, sg_ref[...], lam_init).astype(BF16)


def _attend_sample(rel_table, lam, q, cache_kt, cache_v, k_new, v_new, strip, sg, lam_init):
    batch, t_new, _ = q.shape
    past = cache_kt.shape[2]
    assert past % CHUNK == 0 and t_new <= CHUNK and past > NEAR and t_new <= TQ
    new = pl.BlockSpec((None, t_new, D_MODEL), lambda b: (b, 0, 0))
    return pl.pallas_call(
        functools.partial(_attn_sample_kernel, past=past, t_new=t_new, lam_init=lam_init),
        grid=(batch,),
        in_specs=[pl.BlockSpec(memory_space=pltpu.SMEM), pl.BlockSpec(memory_space=pltpu.SMEM),
                  new,
                  pl.BlockSpec((None, D_MODEL, past), lambda b: (b, 0, 0)),
                  pl.BlockSpec((None, past * N_HEADS, HEAD_DIM), lambda b: (b, 0, 0)),
                  new, new,
                  _const_spec(strip.shape),
                  pl.BlockSpec((1, HEAD_DIM), lambda b: (0, 0))],
        out_specs=new,
        out_shape=jax.ShapeDtypeStruct((batch, t_new, D_MODEL), BF16),
        compiler_params=pltpu.CompilerParams(dimension_semantics=("arbitrary",),
                                             vmem_limit_bytes=VMEM_LIMIT_BYTES),
        name="attn_sample",
    )(rel_table, lam, q, cache_kt, cache_v, k_new, v_new, strip, sg)


def _merge_ffn_kernel(x_ref, att_ref, gu_ref, gv_ref, sa_ref, sg_ref, ws_ref, bt_ref, wab_ref,
                      wgb_ref, wout_ref, g2_ref, wfi_ref, wfo_ref, o_ref, gm_scr, *, tm, tc):
    row = lax.broadcasted_iota(jnp.int32, (GM_CHUNK, GM_CHUNK), 0)
    col = lax.broadcasted_iota(jnp.int32, (GM_CHUNK, GM_CHUNK), 1)
    for g in range(GM_GROUPS):
        cols = slice(g * GM_CHUNK, (g + 1) * GM_CHUNK)
        wm = jnp.where(row >= col, ws_ref[g], 0.0).astype(BF16)[:tc]
        b_col = bt_ref[:tc, g:g + 1]
        for c in range(tm // tc):
            rows = slice(c * tc, (c + 1) * tc)
            vv = gv_ref[rows, cols]
            if tc < GM_CHUNK:
                vv = jnp.concatenate([vv, jnp.zeros((GM_CHUNK - tc, GM_CHUNK), BF16)], axis=0)
            z = jnp.dot(wm, vv, preferred_element_type=F32) + b_col
            gm_scr[rows, cols] = (gu_ref[rows, cols].astype(F32) * z).astype(BF16)

    y = (sa_ref[...].astype(F32) * jnp.dot(att_ref[...], wab_ref[...], preferred_element_type=F32)
         + sg_ref[...].astype(F32) * jnp.dot(gm_scr[...], wgb_ref[...], preferred_element_type=F32))
    x1 = x_ref[...] + jnp.dot(y.astype(BF16), wout_ref[...], preferred_element_type=F32)
    h2 = _rms(x1, g2_ref[...]).astype(BF16)
    gate = jnp.dot(h2, wfi_ref[:, :D_FF], preferred_element_type=F32)
    up = jnp.dot(h2, wfi_ref[:, D_FF:], preferred_element_type=F32)
    act = (jax.nn.silu(gate) * up).astype(BF16)
    o_ref[...] = x1 + jnp.dot(act, wfo_ref[...], preferred_element_type=F32)


def _merge_ffn(x2d, att, gu, gv, sa, sg, w_s, b_t, w_ab, w_gb, w_out, g2, w_fi, w_fo, *, tm, tc):
    n = x2d.shape[0]
    row = pl.BlockSpec((tm, D_MODEL), lambda i: (i, 0))
    consts = [w_s, b_t, w_ab, w_gb, w_out, g2, w_fi, w_fo]
    return pl.pallas_call(
        functools.partial(_merge_ffn_kernel, tm=tm, tc=tc),
        grid=(n // tm,),
        in_specs=[row] * 6 + [_const_spec(c.shape) for c in consts],
        out_specs=row,
        out_shape=jax.ShapeDtypeStruct((n, D_MODEL), F32),
        scratch_shapes=[pltpu.VMEM((tm, D_MODEL), BF16)],
        compiler_params=pltpu.CompilerParams(dimension_semantics=("arbitrary",),
                                             vmem_limit_bytes=VMEM_LIMIT_BYTES),
        name="merge_ffn",
    )(x2d, att, gu, gv, sa, sg, *consts)


def kernel(x_prompt, x_sample, cache_k, cache_v, rel_table, norm1_g, w_in, q_norm_g, k_norm_g,
           lambda_q1, lambda_k1, lambda_q2, lambda_k2, subln_g, gm_norm_g, gm_w_s, gm_b,
           w_attn_branch, w_gmlp_branch, w_out, norm2_g, w_ffn_in, w_ffn_out):
    depth = w_in.shape[0]
    assert depth == 1
    batch, seq, _ = x_prompt.shape
    dec_batch, t_new, _ = x_sample.shape
    past = cache_k.shape[2]
    lam_init = 0.8 - 0.6 * math.exp(-0.3 * 0)

    w_in_b = w_in[0].astype(BF16)
    w_ab = w_attn_branch[0].astype(BF16)
    w_gb = w_gmlp_branch[0].astype(BF16)
    w_o = w_out[0].astype(BF16)
    w_fi = w_ffn_in[0].astype(BF16)
    w_fo = w_ffn_out[0].astype(BF16)
    g1 = norm1_g[0].reshape(1, D_MODEL)
    g2 = norm2_g[0].reshape(1, D_MODEL)
    gmg = gm_norm_g[0].reshape(1, D_MODEL)
    qg = jnp.tile(q_norm_g[0], 2 * N_HEADS).reshape(1, D_MODEL)
    kg = jnp.tile(k_norm_g[0], 2 * N_HEADS).reshape(1, D_MODEL)
    sg = subln_g[0].reshape(1, HEAD_DIM)
    b_t = gm_b[0].T
    group = jnp.arange(D_MODEL) // DK
    gsum = (group[:, None] == jnp.arange(LANES)[None, :]).astype(BF16)
    gbc = jnp.concatenate([gsum.T, gsum.T], axis=0)

    strip, lam = _bias_and_lambda(rel_table, lambda_q1[0], lambda_k1[0], lambda_q2[0], lambda_k2[0],
                                  lam_init)

    q, ktf, ktb, vf, vb, gu, gv, sa, sgt = _project(
        x_prompt, g1, w_in_b, qg, kg, gmg, gsum, gbc, tm=PROJ_ROWS, transpose_k=True, emit_gv_f32=False)
    shp = (batch, seq, D_MODEL)
    att = _attend_prompt(rel_table, lam, q.reshape(shp), ktb, vb.reshape(shp), strip, sg, lam_init)
    xp = x_prompt.reshape(batch * seq, D_MODEL)
    yp = _merge_ffn(xp, att.reshape(batch * seq, D_MODEL), gu, gv, sa, sgt, gm_w_s[0], b_t, w_ab, w_gb,
                    w_o, g2, w_fi, w_fo, tm=PROJ_ROWS, tc=GM_CHUNK)

    xs = x_sample.reshape(1, dec_batch * t_new, D_MODEL)
    q_s, kf_s, kb_s, vf_s, vb_s, gu_s, gv_s, sa_s, sg_s, gvf_s = _project(
        xs, g1, w_in_b, qg, kg, gmg, gsum, gbc, tm=PROJ_ROWS, transpose_k=False, emit_gv_f32=True)
    shs = (dec_batch, t_new, D_MODEL)
    cache_kt = jnp.transpose(cache_k[0], (0, 2, 3, 4, 1)).reshape(dec_batch, D_MODEL, past)
    cache_vr = cache_v[0].reshape(dec_batch, past * N_HEADS, HEAD_DIM)
    att_s = _attend_sample(rel_table, lam, q_s.reshape(shs), cache_kt, cache_vr, kb_s.reshape(shs),
                           vb_s.reshape(shs), strip, sg, lam_init)
    ys = _merge_ffn(xs.reshape(dec_batch * t_new, D_MODEL), att_s.reshape(dec_batch * t_new, D_MODEL),
                    gu_s, gv_s, sa_s, sg_s, gm_w_s[0], b_t, w_ab, w_gb, w_o, g2, w_fi, w_fo,
                    tm=PROJ_ROWS, tc=t_new)

    new_k_prompt = jnp.transpose(ktf.reshape(1, batch, N_HEADS, 2, DK, seq), (0, 1, 5, 2, 3, 4))
    return (yp.reshape(batch, seq, D_MODEL),
            ys.reshape(dec_batch, t_new, D_MODEL),
            new_k_prompt,
            vf.reshape(1, batch, seq, N_HEADS, HEAD_DIM),
            kf_s.reshape(1, dec_batch, t_new, N_HEADS, 2, DK),
            vf_s.reshape(1, dec_batch, t_new, N_HEADS, HEAD_DIM),
            gvf_s.reshape(1, dec_batch, t_new, D_MODEL))
```

```python
import functools
import math

import jax
import jax.numpy as jnp
from jax import lax
from jax.experimental import pallas as pl
from jax.experimental.pallas import tpu as pltpu

F32 = jnp.float32
BF16 = jnp.bfloat16

D_MODEL = 1024
N_HEADS = 8
HEAD_DIM = D_MODEL // N_HEADS
DK = HEAD_DIM // 2
SCALE = DK ** -0.5
LOG2E = math.log2(math.e)
CHUNK = 64
N_BUCKETS = 32
MAX_DIST = 128
GM_CHUNK = 128
GM_GROUPS = 8
D_FF = 2816
EPS = 1e-6
NEG = -1e30

LANES = 128
SUBLANES = 8
VMEM_LIMIT_BYTES = 56 * 1024 * 1024

HALF_BUCKETS = N_BUCKETS // 2
MAX_EXACT = HALF_BUCKETS // 2
LOG_THRESHOLDS = tuple(
    math.ceil(MAX_EXACT * (MAX_DIST / MAX_EXACT) ** (k / (HALF_BUCKETS - MAX_EXACT)) - 1e-9)
    for k in range(1, HALF_BUCKETS - MAX_EXACT))
FAR_BUCKET = HALF_BUCKETS - 1

TQ = 256
NEAR = 128
STRIP_W = NEAR + TQ
FAR_TILE = 512
PROJ_ROWS = 256
assert LOG_THRESHOLDS[-1] <= NEAR + 1 and TQ % CHUNK == 0 and NEAR % LANES == 0


def _rms(x, g):
    return x * lax.rsqrt(jnp.mean(x * x, axis=-1, keepdims=True) + EPS) * g


def _const_spec(shape):
    nd = len(shape)
    return pl.BlockSpec(shape, lambda *_: (0,) * nd, pipeline_mode=pl.Buffered(1))


def _bias_strip(tab_ref, h, query_axis):
    shape = (TQ, STRIP_W) if query_axis == 0 else (STRIP_W, TQ)
    i = lax.broadcasted_iota(jnp.int32, shape, query_axis)
    j = lax.broadcasted_iota(jnp.int32, shape, 1 - query_axis)
    rel = j - NEAR - i
    n = jnp.abs(rel)
    large = jnp.full_like(n, MAX_EXACT)
    for t in LOG_THRESHOLDS:
        large = large + (n >= t).astype(jnp.int32)
    bucket = jnp.where(n < MAX_EXACT, n, large) + jnp.where(rel > 0, HALF_BUCKETS, 0)
    bias = jnp.zeros(shape, F32)
    for b in range(N_BUCKETS):
        bias = jnp.where(bucket == b, tab_ref[b, h], bias)
    jd = j - NEAR
    masked = (j >= NEAR) & (jnp.right_shift(jd, 6) > jnp.right_shift(i, 6))
    return jnp.where(masked, NEG, bias * LOG2E)


def _bias_kernel(tab_ref, lq1_ref, lk1_ref, lq2_ref, lk2_ref, strip_ref, stript_ref, lam_ref, *,
                 lam_init):
    h = pl.program_id(0)
    strip_ref[...] = _bias_strip(tab_ref, h, 0)
    stript_ref[...] = _bias_strip(tab_ref, h, 1)

    e1 = jnp.exp(jnp.sum(lq1_ref[...] * lk1_ref[...], axis=-1, keepdims=True))
    e2 = jnp.exp(jnp.sum(lq2_ref[...] * lk2_ref[...], axis=-1, keepdims=True))
    lam_ref[...] = jnp.broadcast_to(e1 - e2 + lam_init, lam_ref.shape)


def _bias_and_lambda(rel_table, lq1, lk1, lq2, lk2, lam_init):
    vec = pl.BlockSpec((1, DK), lambda h: (0, 0))
    strip, strip_t, lam = pl.pallas_call(
        functools.partial(_bias_kernel, lam_init=lam_init),
        grid=(N_HEADS,),
        in_specs=[pl.BlockSpec(memory_space=pltpu.SMEM), vec, vec, vec, vec],
        out_specs=[pl.BlockSpec((None, TQ, STRIP_W), lambda h: (h, 0, 0)),
                   pl.BlockSpec((None, STRIP_W, TQ), lambda h: (h, 0, 0)),
                   pl.BlockSpec((SUBLANES, LANES), lambda h: (0, 0))],
        out_shape=[jax.ShapeDtypeStruct((N_HEADS, TQ, STRIP_W), F32),
                   jax.ShapeDtypeStruct((N_HEADS, STRIP_W, TQ), F32),
                   jax.ShapeDtypeStruct((SUBLANES, LANES), F32)],
        compiler_params=pltpu.CompilerParams(dimension_semantics=("arbitrary",)),
        name="bias",
    )(rel_table, lq1.reshape(1, DK), lk1.reshape(1, DK), lq2.reshape(1, DK), lk2.reshape(1, DK))
    return strip, strip_t, lam[0:1, 0:1]


def _proj_kernel(x_ref, g1_ref, w_ref, qg_ref, kg_ref, gmg_ref, gsum_ref, gbc_ref,
                 q_ref, kf_ref, kb_ref, vf_ref, vb_ref, gu_ref, gv_ref, sa_ref, sg_ref, *gvf_ref,
                 transposed):
    h = _rms(x_ref[...], g1_ref[...]).astype(BF16)
    maybe_t = (lambda a: a.T) if transposed else (lambda a: a)

    def section(i):
        return jnp.dot(h, w_ref[:, i * D_MODEL:(i + 1) * D_MODEL], preferred_element_type=F32)

    def group_norm(y, g):
        ss = jnp.dot((y * y).astype(BF16), gsum_ref[...], preferred_element_type=F32)
        inv = lax.rsqrt(ss * (1.0 / DK) + EPS)
        hi = inv.astype(BF16)
        lo = (inv - hi.astype(F32)).astype(BF16)
        inv_b = jnp.dot(jnp.concatenate([hi, lo], axis=-1), gbc_ref[...], preferred_element_type=F32)
        return y * inv_b * g

    q_ref[...] = maybe_t(group_norm(section(0), qg_ref[...]) * (SCALE * LOG2E)).astype(BF16)
    k = group_norm(section(1), kg_ref[...])
    kf_ref[...] = maybe_t(k)
    kb_ref[...] = k.astype(BF16)
    v = section(2)
    vf_ref[...] = v
    vb_ref[...] = maybe_t(v).astype(BF16)
    gu_ref[...] = jax.nn.gelu(section(3)).astype(BF16)
    gv = _rms(jax.nn.gelu(section(4)), gmg_ref[...])
    gv_ref[...] = gv.astype(BF16)
    if gvf_ref:
        gvf_ref[0][...] = gv
    sa_ref[...] = jax.nn.sigmoid(section(5)).astype(BF16)
    sg_ref[...] = jax.nn.sigmoid(section(6)).astype(BF16)


def _project(x3d, g1, w_in, qg, kg, gmg, gsum, gbc, *, tm, transposed, emit_gv_f32):
    batch, rows, _ = x3d.shape
    steps = rows // tm
    n = batch * rows
    row = pl.BlockSpec((tm, D_MODEL), lambda b, i: (b * steps + i, 0))
    row_shape = lambda dt: jax.ShapeDtypeStruct((n, D_MODEL), dt)
    if transposed:
        t_spec = pl.BlockSpec((None, D_MODEL, tm), lambda b, i: (b, 0, i))
        t_shape = lambda dt: jax.ShapeDtypeStruct((batch, D_MODEL, rows), dt)
    else:
        t_spec, t_shape = row, row_shape
    out_specs = [t_spec, t_spec, row, row, t_spec] + [row] * (4 + int(emit_gv_f32))
    out_shape = ([t_shape(BF16), t_shape(F32), row_shape(BF16), row_shape(F32), t_shape(BF16)]
                 + [row_shape(BF16)] * 4 + ([row_shape(F32)] if emit_gv_f32 else []))
    return pl.pallas_call(
        functools.partial(_proj_kernel, transposed=transposed),
        grid=(batch, steps),
        in_specs=[pl.BlockSpec((None, tm, D_MODEL), lambda b, i: (b, i, 0)),
                  _const_spec((1, D_MODEL)), _const_spec(w_in.shape), _const_spec((1, D_MODEL)),
                  _const_spec((1, D_MODEL)), _const_spec((1, D_MODEL)), _const_spec(gsum.shape),
                  _const_spec(gbc.shape)],
        out_specs=out_specs,
        out_shape=out_shape,
        compiler_params=pltpu.CompilerParams(dimension_semantics=("arbitrary", "arbitrary"),
                                             vmem_limit_bytes=VMEM_LIMIT_BYTES),
        name="proj",
    )(x3d, g1, w_in, qg, kg, gmg, gsum, gbc)


def _split_maps(q):
    lane = lax.broadcasted_iota(jnp.int32, q.shape, 1)
    zero = jnp.zeros_like(q)
    return jnp.concatenate([jnp.where(lane < DK, q, zero), jnp.where(lane >= DK, q, zero)], axis=0)


def _both_maps(x):
    return jnp.concatenate([x, x], axis=0)


def _with_ones(v):
    lane = lax.broadcasted_iota(jnp.int32, v.shape, 1)
    return jnp.concatenate([v, (lane == 0).astype(v.dtype)], axis=1)


def _finish_heads(ol, lam, sg, lam_init):
    r = ol.shape[0] // 2
    o, l = ol[:, :HEAD_DIM], ol[:, HEAD_DIM:HEAD_DIM + 1]
    d = o[:r] / l[:r] - lam * (o[r:] / l[r:])
    return _rms(d, sg) * (1.0 - lam_init)


def _attn_prompt_kernel(tab_ref, lam_ref, qt_ref, k_ref, vt_ref, stript_ref, sg_ref, o_ref, s_scr,
                        *, seq, lam_init):
    h = pl.program_id(1)
    c_far = tab_ref[FAR_BUCKET, h] * LOG2E
    lam = lam_ref[0, 0]
    dim = lax.broadcasted_iota(jnp.int32, (HEAD_DIM, TQ), 0)

    def values_and_sums(k0, kw, e):
        ones_row = lax.broadcasted_iota(jnp.int32, (2 * SUBLANES, kw), 0) == 0
        vt = jnp.concatenate([vt_ref[:, k0:k0 + kw], ones_row.astype(F32).astype(BF16)], axis=0)
        return jnp.dot(vt, e.astype(BF16), preferred_element_type=F32)

    class Block:
        def __init__(self, qi):
            self.q0 = qi * TQ
            self.end = self.q0 + TQ
            self.s_buf = s_scr.at[qi % 2]
            self.far_end = max(self.q0 - NEAR, 0)
            self.tiles = [(k0, min(FAR_TILE, self.far_end - k0))
                          for k0 in range(0, self.far_end, FAR_TILE)]
            self.m_far = None
            self.ol = None

        def pass1_items(self):
            def prepare():
                qt = qt_ref[:, self.q0:self.end]
                zero = jnp.zeros_like(qt)
                self.w = jnp.concatenate(
                    [jnp.where(dim < DK, qt, zero), jnp.where(dim >= DK, qt, zero)], axis=1)

            def far(k0, kw):
                s = jnp.dot(k_ref[k0:k0 + kw, :], self.w, preferred_element_type=F32)
                self.s_buf[k0:k0 + kw, :] = s
                mt = jnp.max(s, axis=0, keepdims=True)
                self.m_far = mt if self.m_far is None else jnp.maximum(self.m_far, mt)

            def strip():
                bias = stript_ref[STRIP_W - (self.end - self.far_end):, :]
                s = jnp.dot(k_ref[self.far_end:self.end, :], self.w, preferred_element_type=F32)
                s = s + jnp.concatenate([bias, bias], axis=1)
                self.s_buf[self.far_end:self.end, :] = s
                m = jnp.max(s, axis=0, keepdims=True)
                if self.m_far is not None:
                    m = jnp.maximum(m, self.m_far + c_far)
                self.m = m

            return ([prepare] + [functools.partial(far, k0, kw) for k0, kw in self.tiles] + [strip])

        def pass2_items(self):
            def tile(k0, kw, shift):
                part = values_and_sums(k0, kw, jnp.exp2(self.s_buf[k0:k0 + kw, :] - shift))
                self.ol = part if self.ol is None else self.ol + part

            def finish():
                on = self.ol[:HEAD_DIM] / self.ol[HEAD_DIM:HEAD_DIM + 1]
                d = on[:, :TQ] - lam * on[:, TQ:]
                d = d * lax.rsqrt(jnp.mean(d * d, axis=0, keepdims=True) + EPS)
                o_ref[self.q0:self.end, :] = (d.T * sg_ref[...] * (1.0 - lam_init)).astype(BF16)

            items = [lambda: tile(self.far_end, self.end - self.far_end, self.m)]
            items += [functools.partial(lambda k0, kw: tile(k0, kw, self.m - c_far), k0, kw)
                      for k0, kw in self.tiles]
            return items + [finish]

    blocks = [Block(qi) for qi in range(seq // TQ)]
    for item in blocks[0].pass1_items():
        item()
    for qi, blk in enumerate(blocks):
        mine = blk.pass2_items()
        nxt = blocks[qi + 1].pass1_items() if qi + 1 < len(blocks) else []
        for step in range(max(len(mine), len(nxt))):
            if step < len(nxt):
                nxt[step]()
            if step < len(mine):
                mine[step]()


def _attend_prompt(rel_table, lam, qt, k, vt, strip_t, sg, lam_init):
    batch, seq, _ = k.shape
    head = pl.BlockSpec((None, seq, HEAD_DIM), lambda b, h: (b, 0, h))
    head_t = pl.BlockSpec((None, HEAD_DIM, seq), lambda b, h: (b, h, 0))
    return pl.pallas_call(
        functools.partial(_attn_prompt_kernel, seq=seq, lam_init=lam_init),
        grid=(batch, N_HEADS),
        in_specs=[pl.BlockSpec(memory_space=pltpu.SMEM), pl.BlockSpec(memory_space=pltpu.SMEM),
                  head_t, head, head_t,
                  pl.BlockSpec((None, STRIP_W, TQ), lambda b, h: (h, 0, 0)),
                  pl.BlockSpec((1, HEAD_DIM), lambda b, h: (0, 0))],
        out_specs=head,
        out_shape=jax.ShapeDtypeStruct((batch, seq, D_MODEL), BF16),
        scratch_shapes=[pltpu.VMEM((2, seq, 2 * TQ), F32)],
        compiler_params=pltpu.CompilerParams(dimension_semantics=("arbitrary", "arbitrary"),
                                             vmem_limit_bytes=VMEM_LIMIT_BYTES),
        name="attn_prompt",
    )(rel_table, lam, qt, k, vt, strip_t, sg)


def _attn_sample_kernel(tab_ref, lam_ref, q_ref, ckt_ref, cv_ref, kn_ref, vn_ref, strip_ref, sg_ref,
                        o_ref, *, past, t_new, lam_init):
    lam = lam_ref[0, 0]
    far = past - NEAR
    for h in range(N_HEADS):
        cols = slice(h * HEAD_DIM, (h + 1) * HEAD_DIM)
        c_far = tab_ref[FAR_BUCKET, h] * LOG2E
        strip = _both_maps(strip_ref[h, :t_new, :])
        qq = _split_maps(q_ref[:, cols])
        s_c = jnp.dot(qq, ckt_ref[cols, :].astype(BF16), preferred_element_type=F32)
        s_near = s_c[:, far:] + strip[:, :NEAR]
        s_new = lax.dot_general(qq, kn_ref[:, cols], (((1,), (1,)), ((), ())),
                                preferred_element_type=F32) + strip[:, NEAR:NEAR + t_new]
        m = jnp.maximum(jnp.maximum(jnp.max(s_c[:, :far], axis=-1, keepdims=True) + c_far,
                                    jnp.max(s_near, axis=-1, keepdims=True)),
                        jnp.max(s_new, axis=-1, keepdims=True))
        e_far = jnp.exp2(s_c[:, :far] - (m - c_far))
        e_near = jnp.exp2(s_near - m)
        e_new = jnp.exp2(s_new - m)
        cv = _with_ones(cv_ref[pl.ds(h, past, stride=N_HEADS), :].astype(BF16))
        ol = (jnp.dot(e_far.astype(BF16), cv[:far], preferred_element_type=F32)
              + jnp.dot(e_near.astype(BF16), cv[far:], preferred_element_type=F32)
              + jnp.dot(e_new.astype(BF16), _with_ones(vn_ref[:, cols]), preferred_element_type=F32))
        o_ref[:, cols] = _finish_heads(ol, lam, sg_ref[...], lam_init).astype(BF16)


def _attend_sample(rel_table, lam, q, cache_kt, cache_v, k_new, v_new, strip, sg, lam_init):
    batch, t_new, _ = q.shape
    past = cache_kt.shape[2]
    assert past % CHUNK == 0 and t_new <= CHUNK and past > NEAR and t_new <= TQ
    new = pl.BlockSpec((None, t_new, D_MODEL), lambda b: (b, 0, 0))
    return pl.pallas_call(
        functools.partial(_attn_sample_kernel, past=past, t_new=t_new, lam_init=lam_init),
        grid=(batch,),
        in_specs=[pl.BlockSpec(memory_space=pltpu.SMEM), pl.BlockSpec(memory_space=pltpu.SMEM),
                  new,
                  pl.BlockSpec((None, D_MODEL, past), lambda b: (b, 0, 0)),
                  pl.BlockSpec((None, past * N_HEADS, HEAD_DIM), lambda b: (b, 0, 0)),
                  new, new,
                  _const_spec(strip.shape),
                  pl.BlockSpec((1, HEAD_DIM), lambda b: (0, 0))],
        out_specs=new,
        out_shape=jax.ShapeDtypeStruct((batch, t_new, D_MODEL), BF16),
        compiler_params=pltpu.CompilerParams(dimension_semantics=("arbitrary",),
                                             vmem_limit_bytes=VMEM_LIMIT_BYTES),
        name="attn_sample",
    )(rel_table, lam, q, cache_kt, cache_v, k_new, v_new, strip, sg)


def _merge_ffn_kernel(x_ref, att_ref, gu_ref, gv_ref, sa_ref, sg_ref, ws_ref, bt_ref, wab_ref,
                      wgb_ref, wout_ref, g2_ref, wfi_ref, wfo_ref, o_ref, gm_scr, *, tm, tc):
    row = lax.broadcasted_iota(jnp.int32, (GM_CHUNK, GM_CHUNK), 0)
    col = lax.broadcasted_iota(jnp.int32, (GM_CHUNK, GM_CHUNK), 1)
    for g in range(GM_GROUPS):
        cols = slice(g * GM_CHUNK, (g + 1) * GM_CHUNK)
        wm = jnp.where(row >= col, ws_ref[g], 0.0).astype(BF16)[:tc]
        b_col = bt_ref[:tc, g:g + 1]
        for c in range(tm // tc):
            rows = slice(c * tc, (c + 1) * tc)
            vv = gv_ref[rows, cols]
            if tc < GM_CHUNK:
                vv = jnp.concatenate([vv, jnp.zeros((GM_CHUNK - tc, GM_CHUNK), BF16)], axis=0)
            z = jnp.dot(wm, vv, preferred_element_type=F32) + b_col
            gm_scr[rows, cols] = (gu_ref[rows, cols].astype(F32) * z).astype(BF16)

    y = (sa_ref[...].astype(F32) * jnp.dot(att_ref[...], wab_ref[...], preferred_element_type=F32)
         + sg_ref[...].astype(F32) * jnp.dot(gm_scr[...], wgb_ref[...], preferred_element_type=F32))
    x1 = x_ref[...] + jnp.dot(y.astype(BF16), wout_ref[...], preferred_element_type=F32)
    h2 = _rms(x1, g2_ref[...]).astype(BF16)
    gate = jnp.dot(h2, wfi_ref[:, :D_FF], preferred_element_type=F32)
    up = jnp.dot(h2, wfi_ref[:, D_FF:], preferred_element_type=F32)
    act = (jax.nn.silu(gate) * up).astype(BF16)
    o_ref[...] = x1 + jnp.dot(act, wfo_ref[...], preferred_element_type=F32)


def _merge_ffn(x2d, att, gu, gv, sa, sg, w_s, b_t, w_ab, w_gb, w_out, g2, w_fi, w_fo, *, tm, tc):
    n = x2d.shape[0]
    row = pl.BlockSpec((tm, D_MODEL), lambda i: (i, 0))
    consts = [w_s, b_t, w_ab, w_gb, w_out, g2, w_fi, w_fo]
    return pl.pallas_call(
        functools.partial(_merge_ffn_kernel, tm=tm, tc=tc),
        grid=(n // tm,),
        in_specs=[row] * 6 + [_const_spec(c.shape) for c in consts],
        out_specs=row,
        out_shape=jax.ShapeDtypeStruct((n, D_MODEL), F32),
        scratch_shapes=[pltpu.VMEM((tm, D_MODEL), BF16)],
        compiler_params=pltpu.CompilerParams(dimension_semantics=("arbitrary",),
                                             vmem_limit_bytes=VMEM_LIMIT_BYTES),
        name="merge_ffn",
    )(x2d, att, gu, gv, sa, sg, *consts)


def kernel(x_prompt, x_sample, cache_k, cache_v, rel_table, norm1_g, w_in, q_norm_g, k_norm_g,
           lambda_q1, lambda_k1, lambda_q2, lambda_k2, subln_g, gm_norm_g, gm_w_s, gm_b,
           w_attn_branch, w_gmlp_branch, w_out, norm2_g, w_ffn_in, w_ffn_out):
    depth = w_in.shape[0]
    assert depth == 1
    batch, seq, _ = x_prompt.shape
    dec_batch, t_new, _ = x_sample.shape
    past = cache_k.shape[2]
    lam_init = 0.8 - 0.6 * math.exp(-0.3 * 0)

    w_in_b = w_in[0].astype(BF16)
    w_ab = w_attn_branch[0].astype(BF16)
    w_gb = w_gmlp_branch[0].astype(BF16)
    w_o = w_out[0].astype(BF16)
    w_fi = w_ffn_in[0].astype(BF16)
    w_fo = w_ffn_out[0].astype(BF16)
    g1 = norm1_g[0].reshape(1, D_MODEL)
    g2 = norm2_g[0].reshape(1, D_MODEL)
    gmg = gm_norm_g[0].reshape(1, D_MODEL)
    qg = jnp.tile(q_norm_g[0], 2 * N_HEADS).reshape(1, D_MODEL)
    kg = jnp.tile(k_norm_g[0], 2 * N_HEADS).reshape(1, D_MODEL)
    sg = subln_g[0].reshape(1, HEAD_DIM)
    b_t = gm_b[0].T
    group = jnp.arange(D_MODEL) // DK
    gsum = (group[:, None] == jnp.arange(LANES)[None, :]).astype(BF16)
    gbc = jnp.concatenate([gsum.T, gsum.T], axis=0)

    strip, strip_t, lam = _bias_and_lambda(rel_table, lambda_q1[0], lambda_k1[0], lambda_q2[0],
                                           lambda_k2[0], lam_init)

    qt, ktf, kb, vf, vtb, gu, gv, sa, sgt = _project(
        x_prompt, g1, w_in_b, qg, kg, gmg, gsum, gbc, tm=PROJ_ROWS, transposed=True, emit_gv_f32=False)
    att = _attend_prompt(rel_table, lam, qt, kb.reshape(batch, seq, D_MODEL), vtb, strip_t, sg, lam_init)
    xp = x_prompt.reshape(batch * seq, D_MODEL)
    yp = _merge_ffn(xp, att.reshape(batch * seq, D_MODEL), gu, gv, sa, sgt, gm_w_s[0], b_t, w_ab, w_gb,
                    w_o, g2, w_fi, w_fo, tm=PROJ_ROWS, tc=GM_CHUNK)

    xs = x_sample.reshape(1, dec_batch * t_new, D_MODEL)
    q_s, kf_s, kb_s, vf_s, vb_s, gu_s, gv_s, sa_s, sg_s, gvf_s = _project(
        xs, g1, w_in_b, qg, kg, gmg, gsum, gbc, tm=PROJ_ROWS, transposed=False, emit_gv_f32=True)
    shs = (dec_batch, t_new, D_MODEL)
    cache_kt = jnp.transpose(cache_k[0], (0, 2, 3, 4, 1)).reshape(dec_batch, D_MODEL, past)
    cache_vr = cache_v[0].reshape(dec_batch, past * N_HEADS, HEAD_DIM)
    att_s = _attend_sample(rel_table, lam, q_s.reshape(shs), cache_kt, cache_vr, kb_s.reshape(shs),
                           vb_s.reshape(shs), strip, sg, lam_init)
    ys = _merge_ffn(xs.reshape(dec_batch * t_new, D_MODEL), att_s.reshape(dec_batch * t_new, D_MODEL),
                    gu_s, gv_s, sa_s, sg_s, gm_w_s[0], b_t, w_ab, w_gb, w_o, g2, w_fi, w_fo,
                    tm=PROJ_ROWS, tc=t_new)

    new_k_prompt = jnp.transpose(ktf.reshape(1, batch, N_HEADS, 2, DK, seq), (0, 1, 5, 2, 3, 4))
    return (yp.reshape(batch, seq, D_MODEL),
            ys.reshape(dec_batch, t_new, D_MODEL),
            new_k_prompt,
            vf.reshape(1, batch, seq, N_HEADS, HEAD_DIM),
            kf_s.reshape(1, dec_batch, t_new, N_HEADS, 2, DK),
            vf_s.reshape(1, dec_batch, t_new, N_HEADS, HEAD_DIM),
            gvf_s.reshape(1, dec_batch, t_new, D_MODEL))
```

```python
import functools
import math

import jax
import jax.numpy as jnp
from jax import lax
from jax.experimental import pallas as pl
from jax.experimental.pallas import tpu as pltpu

F32 = jnp.float32
BF16 = jnp.bfloat16

D_MODEL = 1024
N_HEADS = 8
HEAD_DIM = D_MODEL // N_HEADS
DK = HEAD_DIM // 2
SCALE = DK ** -0.5
LOG2E = math.log2(math.e)
CHUNK = 64
N_BUCKETS = 32
MAX_DIST = 128
GM_CHUNK = 128
GM_GROUPS = 8
D_FF = 2816
EPS = 1e-6
NEG = -1e30

LANES = 128
SUBLANES = 8
VMEM_LIMIT_BYTES = 56 * 1024 * 1024

HALF_BUCKETS = N_BUCKETS // 2
MAX_EXACT = HALF_BUCKETS // 2
LOG_THRESHOLDS = tuple(
    math.ceil(MAX_EXACT * (MAX_DIST / MAX_EXACT) ** (k / (HALF_BUCKETS - MAX_EXACT)) - 1e-9)
    for k in range(1, HALF_BUCKETS - MAX_EXACT))
FAR_BUCKET = HALF_BUCKETS - 1

TQ = 256
NEAR = 128
STRIP_W = NEAR + TQ
FAR_TILE = 512
PROJ_ROWS = 512
assert LOG_THRESHOLDS[-1] <= NEAR + 1 and TQ % CHUNK == 0 and NEAR % LANES == 0


def _rms(x, g):
    return x * lax.rsqrt(jnp.mean(x * x, axis=-1, keepdims=True) + EPS) * g


def _sigmoid(x):
    return 0.5 * jnp.tanh(0.5 * x) + 0.5


def _const_spec(shape):
    nd = len(shape)
    return pl.BlockSpec(shape, lambda *_: (0,) * nd, pipeline_mode=pl.Buffered(1))


def _bias_strip(tab_ref, h, query_axis):
    shape = (TQ, STRIP_W) if query_axis == 0 else (STRIP_W, TQ)
    i = lax.broadcasted_iota(jnp.int32, shape, query_axis)
    j = lax.broadcasted_iota(jnp.int32, shape, 1 - query_axis)
    rel = j - NEAR - i
    n = jnp.abs(rel)
    large = jnp.full_like(n, MAX_EXACT)
    for t in LOG_THRESHOLDS:
        large = large + (n >= t).astype(jnp.int32)
    bucket = jnp.where(n < MAX_EXACT, n, large) + jnp.where(rel > 0, HALF_BUCKETS, 0)
    bias = jnp.zeros(shape, F32)
    for b in range(N_BUCKETS):
        bias = jnp.where(bucket == b, tab_ref[b, h], bias)
    jd = j - NEAR
    masked = (j >= NEAR) & (jnp.right_shift(jd, 6) > jnp.right_shift(i, 6))
    return jnp.where(masked, NEG, bias * LOG2E)


def _bias_kernel(tab_ref, lq1_ref, lk1_ref, lq2_ref, lk2_ref, strip_ref, stript_ref, lam_ref, *,
                 lam_init):
    h = pl.program_id(0)
    strip_ref[...] = _bias_strip(tab_ref, h, 0)
    stript_ref[...] = _bias_strip(tab_ref, h, 1)

    e1 = jnp.exp(jnp.sum(lq1_ref[...] * lk1_ref[...], axis=-1, keepdims=True))
    e2 = jnp.exp(jnp.sum(lq2_ref[...] * lk2_ref[...], axis=-1, keepdims=True))
    lam_ref[...] = jnp.broadcast_to(e1 - e2 + lam_init, lam_ref.shape)


def _bias_and_lambda(rel_table, lq1, lk1, lq2, lk2, lam_init):
    vec = pl.BlockSpec((1, DK), lambda h: (0, 0))
    strip, strip_t, lam = pl.pallas_call(
        functools.partial(_bias_kernel, lam_init=lam_init),
        grid=(N_HEADS,),
        in_specs=[pl.BlockSpec(memory_space=pltpu.SMEM), vec, vec, vec, vec],
        out_specs=[pl.BlockSpec((None, TQ, STRIP_W), lambda h: (h, 0, 0)),
                   pl.BlockSpec((None, STRIP_W, TQ), lambda h: (h, 0, 0)),
                   pl.BlockSpec((SUBLANES, LANES), lambda h: (0, 0))],
        out_shape=[jax.ShapeDtypeStruct((N_HEADS, TQ, STRIP_W), F32),
                   jax.ShapeDtypeStruct((N_HEADS, STRIP_W, TQ), F32),
                   jax.ShapeDtypeStruct((SUBLANES, LANES), F32)],
        compiler_params=pltpu.CompilerParams(dimension_semantics=("arbitrary",)),
        name="bias",
    )(rel_table, lq1.reshape(1, DK), lk1.reshape(1, DK), lq2.reshape(1, DK), lk2.reshape(1, DK))
    return strip, strip_t, lam[0:1, 0:1]


def _proj_kernel(x_ref, g1_ref, w_ref, qg_ref, kg_ref, gmg_ref, gsum_ref, gbc_ref,
                 q_ref, kf_ref, kb_ref, vf_ref, vb_ref, gu_ref, gv_ref, sa_ref, sg_ref, *gvf_ref,
                 transposed):
    h = _rms(x_ref[...], g1_ref[...]).astype(BF16)
    maybe_t = (lambda a: a.T) if transposed else (lambda a: a)

    def section(i):
        return jnp.dot(h, w_ref[:, i * D_MODEL:(i + 1) * D_MODEL], preferred_element_type=F32)

    def group_norm(y, g):
        ss = jnp.dot((y * y).astype(BF16), gsum_ref[...], preferred_element_type=F32)
        inv = lax.rsqrt(ss * (1.0 / DK) + EPS)
        hi = inv.astype(BF16)
        lo = (inv - hi.astype(F32)).astype(BF16)
        inv_b = jnp.dot(jnp.concatenate([hi, lo], axis=-1), gbc_ref[...], preferred_element_type=F32)
        return y * inv_b * g

    q_ref[...] = maybe_t(group_norm(section(0), qg_ref[...]) * (SCALE * LOG2E)).astype(BF16)
    k = group_norm(section(1), kg_ref[...])
    kf_ref[...] = maybe_t(k)
    kb_ref[...] = k.astype(BF16)
    v = section(2)
    vf_ref[...] = v
    vb_ref[...] = maybe_t(v).astype(BF16)
    gu_ref[...] = jax.nn.gelu(section(3)).astype(BF16)
    gv = _rms(jax.nn.gelu(section(4)), gmg_ref[...])
    gv_ref[...] = gv.astype(BF16)
    if gvf_ref:
        gvf_ref[0][...] = gv
    sa_ref[...] = _sigmoid(section(5)).astype(BF16)
    sg_ref[...] = _sigmoid(section(6)).astype(BF16)


def _project(x3d, g1, w_in, qg, kg, gmg, gsum, gbc, *, tm, transposed, emit_gv_f32):
    batch, rows, _ = x3d.shape
    steps = rows // tm
    n = batch * rows
    row = pl.BlockSpec((tm, D_MODEL), lambda b, i: (b * steps + i, 0))
    row_shape = lambda dt: jax.ShapeDtypeStruct((n, D_MODEL), dt)
    if transposed:
        t_spec = pl.BlockSpec((None, D_MODEL, tm), lambda b, i: (b, 0, i))
        t_shape = lambda dt: jax.ShapeDtypeStruct((batch, D_MODEL, rows), dt)
    else:
        t_spec, t_shape = row, row_shape
    out_specs = [t_spec, t_spec, row, row, t_spec] + [row] * (4 + int(emit_gv_f32))
    out_shape = ([t_shape(BF16), t_shape(F32), row_shape(BF16), row_shape(F32), t_shape(BF16)]
                 + [row_shape(BF16)] * 4 + ([row_shape(F32)] if emit_gv_f32 else []))
    return pl.pallas_call(
        functools.partial(_proj_kernel, transposed=transposed),
        grid=(batch, steps),
        in_specs=[pl.BlockSpec((None, tm, D_MODEL), lambda b, i: (b, i, 0)),
                  _const_spec((1, D_MODEL)), _const_spec(w_in.shape), _const_spec((1, D_MODEL)),
                  _const_spec((1, D_MODEL)), _const_spec((1, D_MODEL)), _const_spec(gsum.shape),
                  _const_spec(gbc.shape)],
        out_specs=out_specs,
        out_shape=out_shape,
        compiler_params=pltpu.CompilerParams(dimension_semantics=("arbitrary", "arbitrary"),
                                             vmem_limit_bytes=VMEM_LIMIT_BYTES),
        name="proj",
    )(x3d, g1, w_in, qg, kg, gmg, gsum, gbc)


def _split_maps(q):
    lane = lax.broadcasted_iota(jnp.int32, q.shape, 1)
    zero = jnp.zeros_like(q)
    return jnp.concatenate([jnp.where(lane < DK, q, zero), jnp.where(lane >= DK, q, zero)], axis=0)


def _both_maps(x):
    return jnp.concatenate([x, x], axis=0)


def _with_ones(v):
    lane = lax.broadcasted_iota(jnp.int32, v.shape, 1)
    return jnp.concatenate([v, (lane == 0).astype(v.dtype)], axis=1)


def _finish_heads(ol, lam, sg, lam_init):
    r = ol.shape[0] // 2
    o, l = ol[:, :HEAD_DIM], ol[:, HEAD_DIM:HEAD_DIM + 1]
    d = o[:r] / l[:r] - lam * (o[r:] / l[r:])
    return _rms(d, sg) * (1.0 - lam_init)


def _attn_prompt_kernel(tab_ref, lam_ref, qt_ref, k_ref, vt_ref, stript_ref, sg_ref, o_ref, s_scr,
                        *, seq, lam_init):
    h = pl.program_id(1)
    c_far = tab_ref[FAR_BUCKET, h] * LOG2E
    lam = lam_ref[0, 0]
    dim = lax.broadcasted_iota(jnp.int32, (HEAD_DIM, TQ), 0)

    def values_and_sums(k0, kw, e):
        ones_row = lax.broadcasted_iota(jnp.int32, (2 * SUBLANES, kw), 0) == 0
        vt = jnp.concatenate([vt_ref[:, k0:k0 + kw], ones_row.astype(F32).astype(BF16)], axis=0)
        return jnp.dot(vt, e.astype(BF16), preferred_element_type=F32)

    class Block:
        def __init__(self, qi):
            self.q0 = qi * TQ
            self.end = self.q0 + TQ
            self.s_buf = s_scr.at[qi % 2]
            self.far_end = max(self.q0 - NEAR, 0)
            self.tiles = [(k0, min(FAR_TILE, self.far_end - k0))
                          for k0 in range(0, self.far_end, FAR_TILE)]
            self.m_far = None
            self.ol = None

        def pass1_items(self):
            def prepare():
                qt = qt_ref[:, self.q0:self.end]
                zero = jnp.zeros_like(qt)
                self.w = jnp.concatenate(
                    [jnp.where(dim < DK, qt, zero), jnp.where(dim >= DK, qt, zero)], axis=1)

            def far(k0, kw):
                s = jnp.dot(k_ref[k0:k0 + kw, :], self.w, preferred_element_type=F32)
                self.s_buf[k0:k0 + kw, :] = s
                mt = jnp.max(s, axis=0, keepdims=True)
                self.m_far = mt if self.m_far is None else jnp.maximum(self.m_far, mt)

            def strip():
                bias = stript_ref[STRIP_W - (self.end - self.far_end):, :]
                s = jnp.dot(k_ref[self.far_end:self.end, :], self.w, preferred_element_type=F32)
                s = s + jnp.concatenate([bias, bias], axis=1)
                self.s_buf[self.far_end:self.end, :] = s
                m = jnp.max(s, axis=0, keepdims=True)
                if self.m_far is not None:
                    m = jnp.maximum(m, self.m_far + c_far)
                self.m = m

            return ([prepare] + [functools.partial(far, k0, kw) for k0, kw in self.tiles] + [strip])

        def pass2_items(self):
            def tile(k0, kw, shift):
                part = values_and_sums(k0, kw, jnp.exp2(self.s_buf[k0:k0 + kw, :] - shift))
                self.ol = part if self.ol is None else self.ol + part

            def finish():
                on = self.ol[:HEAD_DIM] / self.ol[HEAD_DIM:HEAD_DIM + 1]
                d = on[:, :TQ] - lam * on[:, TQ:]
                d = d * lax.rsqrt(jnp.mean(d * d, axis=0, keepdims=True) + EPS)
                o_ref[self.q0:self.end, :] = (d.T * sg_ref[...] * (1.0 - lam_init)).astype(BF16)

            items = [lambda: tile(self.far_end, self.end - self.far_end, self.m)]
            items += [functools.partial(lambda k0, kw: tile(k0, kw, self.m - c_far), k0, kw)
                      for k0, kw in self.tiles]
            return items + [finish]

    blocks = [Block(qi) for qi in range(seq // TQ)]
    for item in blocks[0].pass1_items():
        item()
    for qi, blk in enumerate(blocks):
        mine = blk.pass2_items()
        nxt = blocks[qi + 1].pass1_items() if qi + 1 < len(blocks) else []
        for step in range(max(len(mine), len(nxt))):
            if step < len(nxt):
                nxt[step]()
            if step < len(mine):
                mine[step]()


def _attend_prompt(rel_table, lam, qt, k, vt, strip_t, sg, lam_init):
    batch, seq, _ = k.shape
    head = pl.BlockSpec((None, seq, HEAD_DIM), lambda b, h: (b, 0, h))
    head_t = pl.BlockSpec((None, HEAD_DIM, seq), lambda b, h: (b, h, 0))
    return pl.pallas_call(
        functools.partial(_attn_prompt_kernel, seq=seq, lam_init=lam_init),
        grid=(batch, N_HEADS),
        in_specs=[pl.BlockSpec(memory_space=pltpu.SMEM), pl.BlockSpec(memory_space=pltpu.SMEM),
                  head_t, head, head_t,
                  pl.BlockSpec((None, STRIP_W, TQ), lambda b, h: (h, 0, 0)),
                  pl.BlockSpec((1, HEAD_DIM), lambda b, h: (0, 0))],
        out_specs=head,
        out_shape=jax.ShapeDtypeStruct((batch, seq, D_MODEL), BF16),
        scratch_shapes=[pltpu.VMEM((2, seq, 2 * TQ), F32)],
        compiler_params=pltpu.CompilerParams(dimension_semantics=("arbitrary", "arbitrary"),
                                             vmem_limit_bytes=VMEM_LIMIT_BYTES),
        name="attn_prompt",
    )(rel_table, lam, qt, k, vt, strip_t, sg)


def _attn_sample_kernel(tab_ref, lam_ref, q_ref, ckt_ref, cv_ref, kn_ref, vn_ref, strip_ref, sg_ref,
                        o_ref, *, past, t_new, lam_init):
    lam = lam_ref[0, 0]
    far = past - NEAR
    for h in range(N_HEADS):
        cols = slice(h * HEAD_DIM, (h + 1) * HEAD_DIM)
        c_far = tab_ref[FAR_BUCKET, h] * LOG2E
        strip = _both_maps(strip_ref[h, :t_new, :])
        qq = _split_maps(q_ref[:, cols])
        s_c = jnp.dot(qq, ckt_ref[cols, :].astype(BF16), preferred_element_type=F32)
        s_near = s_c[:, far:] + strip[:, :NEAR]
        s_new = lax.dot_general(qq, kn_ref[:, cols], (((1,), (1,)), ((), ())),
                                preferred_element_type=F32) + strip[:, NEAR:NEAR + t_new]
        m = jnp.maximum(jnp.maximum(jnp.max(s_c[:, :far], axis=-1, keepdims=True) + c_far,
                                    jnp.max(s_near, axis=-1, keepdims=True)),
                        jnp.max(s_new, axis=-1, keepdims=True))
        e_far = jnp.exp2(s_c[:, :far] - (m - c_far))
        e_near = jnp.exp2(s_near - m)
        e_new = jnp.exp2(s_new - m)
        cv = _with_ones(cv_ref[pl.ds(h, past, stride=N_HEADS), :].astype(BF16))
        ol = (jnp.dot(e_far.astype(BF16), cv[:far], preferred_element_type=F32)
              + jnp.dot(e_near.astype(BF16), cv[far:], preferred_element_type=F32)
              + jnp.dot(e_new.astype(BF16), _with_ones(vn_ref[:, cols]), preferred_element_type=F32))
        o_ref[:, cols] = _finish_heads(ol, lam, sg_ref[...], lam_init).astype(BF16)


def _attend_sample(rel_table, lam, q, cache_kt, cache_v, k_new, v_new, strip, sg, lam_init):
    batch, t_new, _ = q.shape
    past = cache_kt.shape[2]
    assert past % CHUNK == 0 and t_new <= CHUNK and past > NEAR and t_new <= TQ
    new = pl.BlockSpec((None, t_new, D_MODEL), lambda b: (b, 0, 0))
    return pl.pallas_call(
        functools.partial(_attn_sample_kernel, past=past, t_new=t_new, lam_init=lam_init),
        grid=(batch,),
        in_specs=[pl.BlockSpec(memory_space=pltpu.SMEM), pl.BlockSpec(memory_space=pltpu.SMEM),
                  new,
                  pl.BlockSpec((None, D_MODEL, past), lambda b: (b, 0, 0)),
                  pl.BlockSpec((None, past * N_HEADS, HEAD_DIM), lambda b: (b, 0, 0)),
                  new, new,
                  _const_spec(strip.shape),
                  pl.BlockSpec((1, HEAD_DIM), lambda b: (0, 0))],
        out_specs=new,
        out_shape=jax.ShapeDtypeStruct((batch, t_new, D_MODEL), BF16),
        compiler_params=pltpu.CompilerParams(dimension_semantics=("arbitrary",),
                                             vmem_limit_bytes=VMEM_LIMIT_BYTES),
        name="attn_sample",
    )(rel_table, lam, q, cache_kt, cache_v, k_new, v_new, strip, sg)


def _merge_ffn_kernel(x_ref, att_ref, gu_ref, gv_ref, sa_ref, sg_ref, ws_ref, bt_ref, wab_ref,
                      wgb_ref, wout_ref, g2_ref, wfi_ref, wfo_ref, o_ref, gm_scr, *, tm, tc):
    row = lax.broadcasted_iota(jnp.int32, (GM_CHUNK, GM_CHUNK), 0)
    col = lax.broadcasted_iota(jnp.int32, (GM_CHUNK, GM_CHUNK), 1)
    for g in range(GM_GROUPS):
        cols = slice(g * GM_CHUNK, (g + 1) * GM_CHUNK)
        wm = jnp.where(row >= col, ws_ref[g], 0.0).astype(BF16)[:tc]
        b_col = bt_ref[:tc, g:g + 1]
        for c in range(tm // tc):
            rows = slice(c * tc, (c + 1) * tc)
            vv = gv_ref[rows, cols]
            if tc < GM_CHUNK:
                vv = jnp.concatenate([vv, jnp.zeros((GM_CHUNK - tc, GM_CHUNK), BF16)], axis=0)
            z = jnp.dot(wm, vv, preferred_element_type=F32) + b_col
            gm_scr[rows, cols] = (gu_ref[rows, cols].astype(F32) * z).astype(BF16)

    y = (sa_ref[...].astype(F32) * jnp.dot(att_ref[...], wab_ref[...], preferred_element_type=F32)
         + sg_ref[...].astype(F32) * jnp.dot(gm_scr[...], wgb_ref[...], preferred_element_type=F32))
    x1 = x_ref[...] + jnp.dot(y.astype(BF16), wout_ref[...], preferred_element_type=F32)
    h2 = _rms(x1, g2_ref[...]).astype(BF16)
    gate = jnp.dot(h2, wfi_ref[:, :D_FF], preferred_element_type=F32)
    up = jnp.dot(h2, wfi_ref[:, D_FF:], preferred_element_type=F32)
    act = (gate * _sigmoid(gate) * up).astype(BF16)
    o_ref[...] = x1 + jnp.dot(act, wfo_ref[...], preferred_element_type=F32)


def _merge_ffn(x2d, att, gu, gv, sa, sg, w_s, b_t, w_ab, w_gb, w_out, g2, w_fi, w_fo, *, tm, tc):
    n = x2d.shape[0]
    row = pl.BlockSpec((tm, D_MODEL), lambda i: (i, 0))
    consts = [w_s, b_t, w_ab, w_gb, w_out, g2, w_fi, w_fo]
    return pl.pallas_call(
        functools.partial(_merge_ffn_kernel, tm=tm, tc=tc),
        grid=(n // tm,),
        in_specs=[row] * 6 + [_const_spec(c.shape) for c in consts],
        out_specs=row,
        out_shape=jax.ShapeDtypeStruct((n, D_MODEL), F32),
        scratch_shapes=[pltpu.VMEM((tm, D_MODEL), BF16)],
        compiler_params=pltpu.CompilerParams(dimension_semantics=("arbitrary",),
                                             vmem_limit_bytes=VMEM_LIMIT_BYTES),
        name="merge_ffn",
    )(x2d, att, gu, gv, sa, sg, *consts)


def kernel(x_prompt, x_sample, cache_k, cache_v, rel_table, norm1_g, w_in, q_norm_g, k_norm_g,
           lambda_q1, lambda_k1, lambda_q2, lambda_k2, subln_g, gm_norm_g, gm_w_s, gm_b,
           w_attn_branch, w_gmlp_branch, w_out, norm2_g, w_ffn_in, w_ffn_out):
    depth = w_in.shape[0]
    assert depth == 1
    batch, seq, _ = x_prompt.shape
    dec_batch, t_new, _ = x_sample.shape
    past = cache_k.shape[2]
    lam_init = 0.8 - 0.6 * math.exp(-0.3 * 0)

    w_in_b = w_in[0].astype(BF16)
    w_ab = w_attn_branch[0].astype(BF16)
    w_gb = w_gmlp_branch[0].astype(BF16)
    w_o = w_out[0].astype(BF16)
    w_fi = w_ffn_in[0].astype(BF16)
    w_fo = w_ffn_out[0].astype(BF16)
    g1 = norm1_g[0].reshape(1, D_MODEL)
    g2 = norm2_g[0].reshape(1, D_MODEL)
    gmg = gm_norm_g[0].reshape(1, D_MODEL)
    qg = jnp.tile(q_norm_g[0], 2 * N_HEADS).reshape(1, D_MODEL)
    kg = jnp.tile(k_norm_g[0], 2 * N_HEADS).reshape(1, D_MODEL)
    sg = subln_g[0].reshape(1, HEAD_DIM)
    b_t = gm_b[0].T
    group = jnp.arange(D_MODEL) // DK
    gsum = (group[:, None] == jnp.arange(LANES)[None, :]).astype(BF16)
    gbc = jnp.concatenate([gsum.T, gsum.T], axis=0)

    strip, strip_t, lam = _bias_and_lambda(rel_table, lambda_q1[0], lambda_k1[0], lambda_q2[0],
                                           lambda_k2[0], lam_init)

    qt, ktf, kb, vf, vtb, gu, gv, sa, sgt = _project(
        x_prompt, g1, w_in_b, qg, kg, gmg, gsum, gbc, tm=PROJ_ROWS, transposed=True, emit_gv_f32=False)
    att = _attend_prompt(rel_table, lam, qt, kb.reshape(batch, seq, D_MODEL), vtb, strip_t, sg, lam_init)
    xp = x_prompt.reshape(batch * seq, D_MODEL)
    yp = _merge_ffn(xp, att.reshape(batch * seq, D_MODEL), gu, gv, sa, sgt, gm_w_s[0], b_t, w_ab, w_gb,
                    w_o, g2, w_fi, w_fo, tm=PROJ_ROWS, tc=GM_CHUNK)

    xs = x_sample.reshape(1, dec_batch * t_new, D_MODEL)
    q_s, kf_s, kb_s, vf_s, vb_s, gu_s, gv_s, sa_s, sg_s, gvf_s = _project(
        xs, g1, w_in_b, qg, kg, gmg, gsum, gbc, tm=PROJ_ROWS, transposed=False, emit_gv_f32=True)
    shs = (dec_batch, t_new, D_MODEL)
    cache_kt = jnp.transpose(cache_k[0], (0, 2, 3, 4, 1)).reshape(dec_batch, D_MODEL, past)
    cache_vr = cache_v[0].reshape(dec_batch, past * N_HEADS, HEAD_DIM)
    att_s = _attend_sample(rel_table, lam, q_s.reshape(shs), cache_kt, cache_vr, kb_s.reshape(shs),
                           vb_s.reshape(shs), strip, sg, lam_init)
    ys = _merge_ffn(xs.reshape(dec_batch * t_new, D_MODEL), att_s.reshape(dec_batch * t_new, D_MODEL),
                    gu_s, gv_s, sa_s, sg_s, gm_w_s[0], b_t, w_ab, w_gb, w_o, g2, w_fi, w_fo,
                    tm=PROJ_ROWS, tc=t_new)

    new_k_prompt = jnp.transpose(ktf.reshape(1, batch, N_HEADS, 2, DK, seq), (0, 1, 5, 2, 3, 4))
    return (yp.reshape(batch, seq, D_MODEL),
            ys.reshape(dec_batch, t_new, D_MODEL),
            new_k_prompt,
            vf.reshape(1, batch, seq, N_HEADS, HEAD_DIM),
            kf_s.reshape(1, dec_batch, t_new, N_HEADS, 2, DK),
            vf_s.reshape(1, dec_batch, t_new, N_HEADS, HEAD_DIM),
            gvf_s.reshape(1, dec_batch, t_new, D_MODEL))
```

```python
import functools
import math

import jax
import jax.numpy as jnp
from jax import lax
from jax.experimental import pallas as pl
from jax.experimental.pallas import tpu as pltpu

F32 = jnp.float32
BF16 = jnp.bfloat16

D_MODEL = 1024
N_HEADS = 8
HEAD_DIM = D_MODEL // N_HEADS
DK = HEAD_DIM // 2
SCALE = DK ** -0.5
LOG2E = math.log2(math.e)
CHUNK = 64
N_BUCKETS = 32
MAX_DIST = 128
GM_CHUNK = 128
GM_GROUPS = 8
D_FF = 2816
EPS = 1e-6
NEG = -1e30

LANES = 128
SUBLANES = 8
VMEM_LIMIT_BYTES = 56 * 1024 * 1024

HALF_BUCKETS = N_BUCKETS // 2
MAX_EXACT = HALF_BUCKETS // 2
LOG_THRESHOLDS = tuple(
    math.ceil(MAX_EXACT * (MAX_DIST / MAX_EXACT) ** (k / (HALF_BUCKETS - MAX_EXACT)) - 1e-9)
    for k in range(1, HALF_BUCKETS - MAX_EXACT))
FAR_BUCKET = HALF_BUCKETS - 1

TQ = 256
NEAR = 128
STRIP_W = NEAR + TQ
FAR_TILE = 512
PROJ_ROWS = 512
assert LOG_THRESHOLDS[-1] <= NEAR + 1 and TQ % CHUNK == 0 and NEAR % LANES == 0


def _rms(x, g):
    return x * lax.rsqrt(jnp.mean(x * x, axis=-1, keepdims=True) + EPS) * g


def _sigmoid(x):
    return 0.5 * jnp.tanh(0.5 * x) + 0.5


def _const_spec(shape):
    nd = len(shape)
    return pl.BlockSpec(shape, lambda *_: (0,) * nd, pipeline_mode=pl.Buffered(1))


def _bias_strip(tab_ref, h, query_axis):
    shape = (TQ, STRIP_W) if query_axis == 0 else (STRIP_W, TQ)
    i = lax.broadcasted_iota(jnp.int32, shape, query_axis)
    j = lax.broadcasted_iota(jnp.int32, shape, 1 - query_axis)
    rel = j - NEAR - i
    n = jnp.abs(rel)
    large = jnp.full_like(n, MAX_EXACT)
    for t in LOG_THRESHOLDS:
        large = large + (n >= t).astype(jnp.int32)
    bucket = jnp.where(n < MAX_EXACT, n, large) + jnp.where(rel > 0, HALF_BUCKETS, 0)
    bias = jnp.zeros(shape, F32)
    for b in range(N_BUCKETS):
        bias = jnp.where(bucket == b, tab_ref[b, h], bias)
    jd = j - NEAR
    masked = (j >= NEAR) & (jnp.right_shift(jd, 6) > jnp.right_shift(i, 6))
    return jnp.where(masked, NEG, bias * LOG2E)


def _bias_kernel(tab_ref, lq1_ref, lk1_ref, lq2_ref, lk2_ref, strip_ref, stript_ref, lam_ref, *,
                 lam_init):
    h = pl.program_id(0)
    strip_ref[...] = _bias_strip(tab_ref, h, 0)
    stript_ref[...] = _bias_strip(tab_ref, h, 1)

    e1 = jnp.exp(jnp.sum(lq1_ref[...] * lk1_ref[...], axis=-1, keepdims=True))
    e2 = jnp.exp(jnp.sum(lq2_ref[...] * lk2_ref[...], axis=-1, keepdims=True))
    lam_ref[...] = jnp.broadcast_to(e1 - e2 + lam_init, lam_ref.shape)


def _bias_and_lambda(rel_table, lq1, lk1, lq2, lk2, lam_init):
    vec = pl.BlockSpec((1, DK), lambda h: (0, 0))
    strip, strip_t, lam = pl.pallas_call(
        functools.partial(_bias_kernel, lam_init=lam_init),
        grid=(N_HEADS,),
        in_specs=[pl.BlockSpec(memory_space=pltpu.SMEM), vec, vec, vec, vec],
        out_specs=[pl.BlockSpec((None, TQ, STRIP_W), lambda h: (h, 0, 0)),
                   pl.BlockSpec((None, STRIP_W, TQ), lambda h: (h, 0, 0)),
                   pl.BlockSpec((SUBLANES, LANES), lambda h: (0, 0))],
        out_shape=[jax.ShapeDtypeStruct((N_HEADS, TQ, STRIP_W), F32),
                   jax.ShapeDtypeStruct((N_HEADS, STRIP_W, TQ), F32),
                   jax.ShapeDtypeStruct((SUBLANES, LANES), F32)],
        compiler_params=pltpu.CompilerParams(dimension_semantics=("arbitrary",)),
        name="bias",
    )(rel_table, lq1.reshape(1, DK), lk1.reshape(1, DK), lq2.reshape(1, DK), lk2.reshape(1, DK))
    return strip, strip_t, lam[0:1, 0:1]


def _proj_kernel(x_ref, g1_ref, w_ref, qg_ref, kg_ref, gmg_ref, gsum_ref, gbc_ref,
                 q_ref, kf_ref, kb_ref, vf_ref, vb_ref, gu_ref, gv_ref, sa_ref, sg_ref, *gvf_ref,
                 transposed):
    h = _rms(x_ref[...], g1_ref[...]).astype(BF16)
    maybe_t = (lambda a: a.T) if transposed else (lambda a: a)

    def section(i):
        return jnp.dot(h, w_ref[:, i * D_MODEL:(i + 1) * D_MODEL], preferred_element_type=F32)

    def group_norm(y, g):
        ss = jnp.dot((y * y).astype(BF16), gsum_ref[...], preferred_element_type=F32)
        inv = lax.rsqrt(ss * (1.0 / DK) + EPS)
        hi = inv.astype(BF16)
        lo = (inv - hi.astype(F32)).astype(BF16)
        inv_b = jnp.dot(jnp.concatenate([hi, lo], axis=-1), gbc_ref[...], preferred_element_type=F32)
        return y * inv_b * g

    q_ref[...] = maybe_t(group_norm(section(0), qg_ref[...]) * (SCALE * LOG2E)).astype(BF16)
    k = group_norm(section(1), kg_ref[...])
    kf_ref[...] = maybe_t(k)
    kb_ref[...] = k.astype(BF16)
    v = section(2)
    vf_ref[...] = v
    vb_ref[...] = maybe_t(v).astype(BF16)
    gu_ref[...] = jax.nn.gelu(section(3)).astype(BF16)
    gv = _rms(jax.nn.gelu(section(4)), gmg_ref[...])
    gv_ref[...] = gv.astype(BF16)
    if gvf_ref:
        gvf_ref[0][...] = gv
    sa_ref[...] = _sigmoid(section(5)).astype(BF16)
    sg_ref[...] = _sigmoid(section(6)).astype(BF16)


def _project(x3d, g1, w_in, qg, kg, gmg, gsum, gbc, *, tm, transposed, emit_gv_f32):
    batch, rows, _ = x3d.shape
    steps = rows // tm
    n = batch * rows
    row = pl.BlockSpec((tm, D_MODEL), lambda b, i: (b * steps + i, 0))
    row_shape = lambda dt: jax.ShapeDtypeStruct((n, D_MODEL), dt)
    if transposed:
        t_spec = pl.BlockSpec((None, D_MODEL, tm), lambda b, i: (b, 0, i))
        t_shape = lambda dt: jax.ShapeDtypeStruct((batch, D_MODEL, rows), dt)
    else:
        t_spec, t_shape = row, row_shape
    out_specs = [t_spec, t_spec, row, row, t_spec] + [row] * (4 + int(emit_gv_f32))
    out_shape = ([t_shape(BF16), t_shape(F32), row_shape(BF16), row_shape(F32), t_shape(BF16)]
                 + [row_shape(BF16)] * 4 + ([row_shape(F32)] if emit_gv_f32 else []))
    return pl.pallas_call(
        functools.partial(_proj_kernel, transposed=transposed),
        grid=(batch, steps),
        in_specs=[pl.BlockSpec((None, tm, D_MODEL), lambda b, i: (b, i, 0)),
                  _const_spec((1, D_MODEL)), _const_spec(w_in.shape), _const_spec((1, D_MODEL)),
                  _const_spec((1, D_MODEL)), _const_spec((1, D_MODEL)), _const_spec(gsum.shape),
                  _const_spec(gbc.shape)],
        out_specs=out_specs,
        out_shape=out_shape,
        compiler_params=pltpu.CompilerParams(dimension_semantics=("arbitrary", "arbitrary"),
                                             vmem_limit_bytes=VMEM_LIMIT_BYTES),
        name="proj",
    )(x3d, g1, w_in, qg, kg, gmg, gsum, gbc)


def _split_maps(q):
    lane = lax.broadcasted_iota(jnp.int32, q.shape, 1)
    zero = jnp.zeros_like(q)
    return jnp.concatenate([jnp.where(lane < DK, q, zero), jnp.where(lane >= DK, q, zero)], axis=0)


def _both_maps(x):
    return jnp.concatenate([x, x], axis=0)


def _with_ones(v):
    lane = lax.broadcasted_iota(jnp.int32, v.shape, 1)
    return jnp.concatenate([v, (lane == 0).astype(v.dtype)], axis=1)


def _finish_heads(ol, lam, sg, lam_init):
    r = ol.shape[0] // 2
    o, l = ol[:, :HEAD_DIM], ol[:, HEAD_DIM:HEAD_DIM + 1]
    d = o[:r] / l[:r] - lam * (o[r:] / l[r:])
    return _rms(d, sg) * (1.0 - lam_init)


def _attn_prompt_kernel(tab_ref, lam_ref, qt_ref, k_ref, vt_ref, stript_ref, sg_ref, o_ref, s_scr,
                        *, seq, lam_init):
    h = pl.program_id(1)
    c_far = tab_ref[FAR_BUCKET, h] * LOG2E
    lam = lam_ref[0, 0]
    dim = lax.broadcasted_iota(jnp.int32, (HEAD_DIM, TQ), 0)

    class Block:
        def __init__(self, qi):
            self.q0 = qi * TQ
            self.end = self.q0 + TQ
            self.s_buf = s_scr.at[qi % 2]
            self.far_end = max(self.q0 - NEAR, 0)
            self.tiles = [(k0, min(FAR_TILE, self.far_end - k0))
                          for k0 in range(0, self.far_end, FAR_TILE)]
            self.m_far = None
            self.ol = None

        def pass1_items(self):
            def prepare():
                qt = qt_ref[:, self.q0:self.end]
                zero = jnp.zeros_like(qt)
                self.w = jnp.concatenate(
                    [jnp.where(dim < DK, qt, zero), jnp.where(dim >= DK, qt, zero)], axis=1)

            def far(k0, kw):
                s = jnp.dot(k_ref[k0:k0 + kw, :], self.w, preferred_element_type=F32)
                self.s_buf[k0:k0 + kw, :] = s
                mt = jnp.max(s, axis=0, keepdims=True)
                self.m_far = mt if self.m_far is None else jnp.maximum(self.m_far, mt)

            def strip():
                bias = stript_ref[STRIP_W - (self.end - self.far_end):, :]
                s = jnp.dot(k_ref[self.far_end:self.end, :], self.w, preferred_element_type=F32)
                s = s + jnp.concatenate([bias, bias], axis=1)
                self.s_buf[self.far_end:self.end, :] = s
                m = jnp.max(s, axis=0, keepdims=True)
                if self.m_far is not None:
                    m = jnp.maximum(m, self.m_far + c_far)
                self.m = m

            return ([prepare] + [functools.partial(far, k0, kw) for k0, kw in self.tiles] + [strip])

        def pass2_items(self):
            def expo(k0, kw, shift):
                e = jnp.exp2(self.s_buf[k0:k0 + kw, :] - shift)
                self.s_buf[k0:k0 + kw, :] = e
                part = jnp.sum(e, axis=0, keepdims=True)
                self.ol = part if self.ol is None else self.ol + part

            def scales():
                r = 1.0 / self.ol
                self.r1, self.r2 = r[:, :TQ], lam * r[:, TQ:]
                self.o = None

            def values(k0, kw):
                e = self.s_buf[k0:k0 + kw, :]
                a = (e[:, :TQ] * self.r1 - e[:, TQ:] * self.r2).astype(BF16)
                part = jnp.dot(vt_ref[:, k0:k0 + kw], a, preferred_element_type=F32)
                self.o = part if self.o is None else self.o + part

            def finish():
                d = self.o
                d = d * lax.rsqrt(jnp.mean(d * d, axis=0, keepdims=True) + EPS)
                o_ref[self.q0:self.end, :] = (d.T * sg_ref[...] * (1.0 - lam_init)).astype(BF16)

            all_tiles = [(self.far_end, self.end - self.far_end)] + self.tiles
            items = [lambda: expo(self.far_end, self.end - self.far_end, self.m)]
            items += [functools.partial(lambda k0, kw: expo(k0, kw, self.m - c_far), k0, kw)
                      for k0, kw in self.tiles]
            items += [scales] + [functools.partial(values, k0, kw) for k0, kw in all_tiles]
            return items + [finish]

    blocks = [Block(qi) for qi in range(seq // TQ)]
    for item in blocks[0].pass1_items():
        item()
    for qi, blk in enumerate(blocks):
        mine = blk.pass2_items()
        nxt = blocks[qi + 1].pass1_items() if qi + 1 < len(blocks) else []
        for step in range(max(len(mine), len(nxt))):
            if step < len(nxt):
                nxt[step]()
            if step < len(mine):
                mine[step]()


def _attend_prompt(rel_table, lam, qt, k, vt, strip_t, sg, lam_init):
    batch, seq, _ = k.shape
    head = pl.BlockSpec((None, seq, HEAD_DIM), lambda b, h: (b, 0, h))
    head_t = pl.BlockSpec((None, HEAD_DIM, seq), lambda b, h: (b, h, 0))
    return pl.pallas_call(
        functools.partial(_attn_prompt_kernel, seq=seq, lam_init=lam_init),
        grid=(batch, N_HEADS),
        in_specs=[pl.BlockSpec(memory_space=pltpu.SMEM), pl.BlockSpec(memory_space=pltpu.SMEM),
                  head_t, head, head_t,
                  pl.BlockSpec((None, STRIP_W, TQ), lambda b, h: (h, 0, 0)),
                  pl.BlockSpec((1, HEAD_DIM), lambda b, h: (0, 0))],
        out_specs=head,
        out_shape=jax.ShapeDtypeStruct((batch, seq, D_MODEL), BF16),
        scratch_shapes=[pltpu.VMEM((2, seq, 2 * TQ), F32)],
        compiler_params=pltpu.CompilerParams(dimension_semantics=("arbitrary", "arbitrary"),
                                             vmem_limit_bytes=VMEM_LIMIT_BYTES),
        name="attn_prompt",
    )(rel_table, lam, qt, k, vt, strip_t, sg)


def _attn_sample_kernel(tab_ref, lam_ref, q_ref, ckt_ref, cv_ref, kn_ref, vn_ref, strip_ref, sg_ref,
                        o_ref, *, past, t_new, lam_init):
    lam = lam_ref[0, 0]
    far = past - NEAR
    for h in range(N_HEADS):
        cols = slice(h * HEAD_DIM, (h + 1) * HEAD_DIM)
        c_far = tab_ref[FAR_BUCKET, h] * LOG2E
        strip = _both_maps(strip_ref[h, :t_new, :])
        qq = _split_maps(q_ref[:, cols])
        s_c = jnp.dot(qq, ckt_ref[cols, :].astype(BF16), preferred_element_type=F32)
        s_near = s_c[:, far:] + strip[:, :NEAR]
        s_new = lax.dot_general(qq, kn_ref[:, cols], (((1,), (1,)), ((), ())),
                                preferred_element_type=F32) + strip[:, NEAR:NEAR + t_new]
        m = jnp.maximum(jnp.maximum(jnp.max(s_c[:, :far], axis=-1, keepdims=True) + c_far,
                                    jnp.max(s_near, axis=-1, keepdims=True)),
                        jnp.max(s_new, axis=-1, keepdims=True))
        e_far = jnp.exp2(s_c[:, :far] - (m - c_far))
        e_near = jnp.exp2(s_near - m)
        e_new = jnp.exp2(s_new - m)
        cv = _with_ones(cv_ref[pl.ds(h, past, stride=N_HEADS), :].astype(BF16))
        ol = (jnp.dot(e_far.astype(BF16), cv[:far], preferred_element_type=F32)
              + jnp.dot(e_near.astype(BF16), cv[far:], preferred_element_type=F32)
              + jnp.dot(e_new.astype(BF16), _with_ones(vn_ref[:, cols]), preferred_element_type=F32))
        o_ref[:, cols] = _finish_heads(ol, lam, sg_ref[...], lam_init).astype(BF16)


def _attend_sample(rel_table, lam, q, cache_kt, cache_v, k_new, v_new, strip, sg, lam_init):
    batch, t_new, _ = q.shape
    past = cache_kt.shape[2]
    assert past % CHUNK == 0 and t_new <= CHUNK and past > NEAR and t_new <= TQ
    new = pl.BlockSpec((None, t_new, D_MODEL), lambda b: (b, 0, 0))
    return pl.pallas_call(
        functools.partial(_attn_sample_kernel, past=past, t_new=t_new, lam_init=lam_init),
        grid=(batch,),
        in_specs=[pl.BlockSpec(memory_space=pltpu.SMEM), pl.BlockSpec(memory_space=pltpu.SMEM),
                  new,
                  pl.BlockSpec((None, D_MODEL, past), lambda b: (b, 0, 0)),
                  pl.BlockSpec((None, past * N_HEADS, HEAD_DIM), lambda b: (b, 0, 0)),
                  new, new,
                  _const_spec(strip.shape),
                  pl.BlockSpec((1, HEAD_DIM), lambda b: (0, 0))],
        out_specs=new,
        out_shape=jax.ShapeDtypeStruct((batch, t_new, D_MODEL), BF16),
        compiler_params=pltpu.CompilerParams(dimension_semantics=("arbitrary",),
                                             vmem_limit_bytes=VMEM_LIMIT_BYTES),
        name="attn_sample",
    )(rel_table, lam, q, cache_kt, cache_v, k_new, v_new, strip, sg)


def _merge_ffn_kernel(x_ref, att_ref, gu_ref, gv_ref, sa_ref, sg_ref, ws_ref, bt_ref, wab_ref,
                      wgb_ref, wout_ref, g2_ref, wfi_ref, wfo_ref, o_ref, gm_scr, *, tm, tc):
    row = lax.broadcasted_iota(jnp.int32, (GM_CHUNK, GM_CHUNK), 0)
    col = lax.broadcasted_iota(jnp.int32, (GM_CHUNK, GM_CHUNK), 1)
    for g in range(GM_GROUPS):
        cols = slice(g * GM_CHUNK, (g + 1) * GM_CHUNK)
        wm = jnp.where(row >= col, ws_ref[g], 0.0).astype(BF16)[:tc]
        b_col = bt_ref[:tc, g:g + 1]
        for c in range(tm // tc):
            rows = slice(c * tc, (c + 1) * tc)
            vv = gv_ref[rows, cols]
            if tc < GM_CHUNK:
                vv = jnp.concatenate([vv, jnp.zeros((GM_CHUNK - tc, GM_CHUNK), BF16)], axis=0)
            z = jnp.dot(wm, vv, preferred_element_type=F32) + b_col
            gm_scr[rows, cols] = (gu_ref[rows, cols].astype(F32) * z).astype(BF16)

    y = (sa_ref[...].astype(F32) * jnp.dot(att_ref[...], wab_ref[...], preferred_element_type=F32)
         + sg_ref[...].astype(F32) * jnp.dot(gm_scr[...], wgb_ref[...], preferred_element_type=F32))
    x1 = x_ref[...] + jnp.dot(y.astype(BF16), wout_ref[...], preferred_element_type=F32)
    h2 = _rms(x1, g2_ref[...]).astype(BF16)
    gate = jnp.dot(h2, wfi_ref[:, :D_FF], preferred_element_type=F32)
    up = jnp.dot(h2, wfi_ref[:, D_FF:], preferred_element_type=F32)
    act = (gate * _sigmoid(gate) * up).astype(BF16)
    o_ref[...] = x1 + jnp.dot(act, wfo_ref[...], preferred_element_type=F32)


def _merge_ffn(x2d, att, gu, gv, sa, sg, w_s, b_t, w_ab, w_gb, w_out, g2, w_fi, w_fo, *, tm, tc):
    n = x2d.shape[0]
    row = pl.BlockSpec((tm, D_MODEL), lambda i: (i, 0))
    consts = [w_s, b_t, w_ab, w_gb, w_out, g2, w_fi, w_fo]
    return pl.pallas_call(
        functools.partial(_merge_ffn_kernel, tm=tm, tc=tc),
        grid=(n // tm,),
        in_specs=[row] * 6 + [_const_spec(c.shape) for c in consts],
        out_specs=row,
        out_shape=jax.ShapeDtypeStruct((n, D_MODEL), F32),
        scratch_shapes=[pltpu.VMEM((tm, D_MODEL), BF16)],
        compiler_params=pltpu.CompilerParams(dimension_semantics=("arbitrary",),
                                             vmem_limit_bytes=VMEM_LIMIT_BYTES),
        name="merge_ffn",
    )(x2d, att, gu, gv, sa, sg, *consts)


def kernel(x_prompt, x_sample, cache_k, cache_v, rel_table, norm1_g, w_in, q_norm_g, k_norm_g,
           lambda_q1, lambda_k1, lambda_q2, lambda_k2, subln_g, gm_norm_g, gm_w_s, gm_b,
           w_attn_branch, w_gmlp_branch, w_out, norm2_g, w_ffn_in, w_ffn_out):
    depth = w_in.shape[0]
    assert depth == 1
    batch, seq, _ = x_prompt.shape
    dec_batch, t_new, _ = x_sample.shape
    past = cache_k.shape[2]
    lam_init = 0.8 - 0.6 * math.exp(-0.3 * 0)

    w_in_b = w_in[0].astype(BF16)
    w_ab = w_attn_branch[0].astype(BF16)
    w_gb = w_gmlp_branch[0].astype(BF16)
    w_o = w_out[0].astype(BF16)
    w_fi = w_ffn_in[0].astype(BF16)
    w_fo = w_ffn_out[0].astype(BF16)
    g1 = norm1_g[0].reshape(1, D_MODEL)
    g2 = norm2_g[0].reshape(1, D_MODEL)
    gmg = gm_norm_g[0].reshape(1, D_MODEL)
    qg = jnp.tile(q_norm_g[0], 2 * N_HEADS).reshape(1, D_MODEL)
    kg = jnp.tile(k_norm_g[0], 2 * N_HEADS).reshape(1, D_MODEL)
    sg = subln_g[0].reshape(1, HEAD_DIM)
    b_t = gm_b[0].T
    group = jnp.arange(D_MODEL) // DK
    gsum = (group[:, None] == jnp.arange(LANES)[None, :]).astype(BF16)
    gbc = jnp.concatenate([gsum.T, gsum.T], axis=0)

    strip, strip_t, lam = _bias_and_lambda(rel_table, lambda_q1[0], lambda_k1[0], lambda_q2[0],
                                           lambda_k2[0], lam_init)

    qt, ktf, kb, vf, vtb, gu, gv, sa, sgt = _project(
        x_prompt, g1, w_in_b, qg, kg, gmg, gsum, gbc, tm=PROJ_ROWS, transposed=True, emit_gv_f32=False)
    att = _attend_prompt(rel_table, lam, qt, kb.reshape(batch, seq, D_MODEL), vtb, strip_t, sg, lam_init)
    xp = x_prompt.reshape(batch * seq, D_MODEL)
    yp = _merge_ffn(xp, att.reshape(batch * seq, D_MODEL), gu, gv, sa, sgt, gm_w_s[0], b_t, w_ab, w_gb,
                    w_o, g2, w_fi, w_fo, tm=PROJ_ROWS, tc=GM_CHUNK)

    xs = x_sample.reshape(1, dec_batch * t_new, D_MODEL)
    q_s, kf_s, kb_s, vf_s, vb_s, gu_s, gv_s, sa_s, sg_s, gvf_s = _project(
        xs, g1, w_in_b, qg, kg, gmg, gsum, gbc, tm=PROJ_ROWS, transposed=False, emit_gv_f32=True)
    shs = (dec_batch, t_new, D_MODEL)
    cache_kt = jnp.transpose(cache_k[0], (0, 2, 3, 4, 1)).reshape(dec_batch, D_MODEL, past)
    cache_vr = cache_v[0].reshape(dec_batch, past * N_HEADS, HEAD_DIM)
    att_s = _attend_sample(rel_table, lam, q_s.reshape(shs), cache_kt, cache_vr, kb_s.reshape(shs),
                           vb_s.reshape(shs), strip, sg, lam_init)
    ys = _merge_ffn(xs.reshape(dec_batch * t_new, D_MODEL), att_s.reshape(dec_batch * t_new, D_MODEL),
                    gu_s, gv_s, sa_s, sg_s, gm_w_s[0], b_t, w_ab, w_gb, w_o, g2, w_fi, w_fo,
                    tm=PROJ_ROWS, tc=t_new)

    new_k_prompt = jnp.transpose(ktf.reshape(1, batch, N_HEADS, 2, DK, seq), (0, 1, 5, 2, 3, 4))
    return (yp.reshape(batch, seq, D_MODEL),
            ys.reshape(dec_batch, t_new, D_MODEL),
            new_k_prompt,
            vf.reshape(1, batch, seq, N_HEADS, HEAD_DIM),
            kf_s.reshape(1, dec_batch, t_new, N_HEADS, 2, DK),
            vf_s.reshape(1, dec_batch, t_new, N_HEADS, HEAD_DIM),
            gvf_s.reshape(1, dec_batch, t_new, D_MODEL))
```
